```python
import math
import jax
import jax.numpy as jnp
from jax import lax
import numpy as np

D_MODEL = 2048
BATCH = 4
SEQ = 4096
DEPTH = 2

GRID_W = 64
CTX_LEN = 256

DA_HEADS = 4
DA_HALF = 64
DA_V = 2 * DA_HALF
ML_HEADS = 4
ML_DK = 128
ML_DV = 128
ML_CHUNK = 64
ML_CONV = 3
MLA_HEADS = 8
MLA_Q_RANK = 512
MLA_KV_RANK = 256
MLA_NOPE = 128
MLA_ROPE = 64
MLA_V = 128
MLA_QK = MLA_NOPE + MLA_ROPE

ROPE_DIM = 64
ROPE_BASE = 10000.0
Q_BLOCK = 128
EPS = 1e-6

MIX_W = DA_HEADS * DA_V + ML_HEADS * ML_DV + MLA_HEADS * MLA_V
FF_HIDDEN = -(-8 * D_MODEL // (3 * 256)) * 256
IN_SIZES = (
    DA_HEADS * 2 * DA_HALF, DA_HEADS * 2 * DA_HALF, DA_HEADS * DA_V,
    ML_HEADS * ML_DK, ML_HEADS * ML_DK, ML_HEADS * ML_DV, ML_HEADS * ML_DV,
    4 * ML_HEADS,
    MLA_Q_RANK, MLA_KV_RANK, MLA_ROPE,
)
IN_W = sum(IN_SIZES)

kernel_name = 'hybrid_diffattn_mlstm_mla_dit'


def rms_norm(x, g):
    xf = x.astype(jnp.float32)
    y = xf * lax.rsqrt(jnp.mean(xf * xf, axis=-1, keepdims=True) + EPS)
    return (y * g.astype(jnp.float32)).astype(x.dtype)


def modulate(x, g, shift, scale):
    return rms_norm(x, g) * (1.0 + scale) + shift


def split_cols(p):
    parts, off = [], 0
    for size in IN_SIZES:
        parts.append(p[..., off:off + size])
        off += size
    return parts


def to_heads(t, n_heads):
    b, t_len, _ = t.shape
    return t.reshape(b, t_len, n_heads, -1).transpose(0, 2, 1, 3)


def from_heads(t):
    b, h, t_len, d = t.shape
    return t.transpose(0, 2, 1, 3).reshape(b, t_len, h * d)


def rope_tables(n_lat):
    rows = n_lat // GRID_W
    r, col = jnp.meshgrid(jnp.arange(rows, dtype=jnp.float32),
                          jnp.arange(GRID_W, dtype=jnp.float32), indexing='ij')
    half = ROPE_DIM // 4
    inv = ROPE_BASE ** (-jnp.arange(half, dtype=jnp.float32) / half)
    ang_r = r.reshape(-1)[:, None] * inv
    ang_c = col.reshape(-1)[:, None] * inv
    return (jnp.cos(ang_r), jnp.sin(ang_r), jnp.cos(ang_c), jnp.sin(ang_c))


def _rotate(x, cos, sin):
    x1, x2 = jnp.split(x, 2, axis=-1)
    return jnp.concatenate([x1 * cos - x2 * sin, x2 * cos + x1 * sin], axis=-1)


def rope_2d(x, tabs):
    cr, sr, cc, sc = tabs
    xr, xc = jnp.split(x, 2, axis=-1)
    return jnp.concatenate([_rotate(xr, cr, sr), _rotate(xc, cc, sc)], axis=-1)


def sweep_query_blocks(fn, q):
    *lead, t_len, d = q.shape
    nb = t_len // Q_BLOCK
    qb = jnp.moveaxis(q.reshape(*lead, nb, Q_BLOCK, d), -3, 0)
    out = jnp.moveaxis(lax.map(fn, qb), 0, -3)
    return out.reshape(*out.shape[:-3], nb * Q_BLOCK, out.shape[-1])


def dwconv_centred(x, w, b):
    k = w.shape[0]
    pad = k // 2
    t_len = x.shape[1]
    xp = jnp.pad(x, ((0, 0), (pad, pad), (0, 0)))
    y = b
    for j in range(k):
        y = y + xp[:, j:j + t_len] * w[j]
    return y


def swiglu(h, w_gu, w_down):
    gate, up = jnp.split(h @ w_gu, 2, axis=-1)
    return (jax.nn.silu(gate) * up) @ w_down


def mlstm_chunkwise(q, k, v, ig, lf, state):
    b_sz, h_sz, t_len, _ = q.shape
    dv = v.shape[-1]
    nc = t_len // ML_CHUNK

    def chunks(t):
        return jnp.moveaxis(t.reshape(b_sz, h_sz, nc, ML_CHUNK, *t.shape[3:]), 2, 0)

    lower = jnp.tril(jnp.ones((ML_CHUNK, ML_CHUNK), dtype=bool))

    def step(carry, inp):
        c_mat, n_vec, m = carry
        qc, kc, vc, igc, lfc = inp
        bcum = jnp.cumsum(lfc, axis=-1)
        log_d = jnp.where(lower, bcum[..., :, None] - bcum[..., None, :] + igc[..., None, :], -jnp.inf)
        log_prev = bcum + m[..., None]
        m_t = jnp.maximum(log_prev, jnp.max(log_d, axis=-1))
        dmat = jnp.exp(log_d - m_t[..., None])
        w_prev = jnp.exp(log_prev - m_t)
        s = jnp.einsum('bhtd,bhsd->bhts', qc, kc) * dmat
        num = jnp.einsum('bhts,bhsv->bhtv', s, vc) + w_prev[..., None] * jnp.einsum('bhtd,bhdv->bhtv', qc, c_mat)
        den = jnp.sum(s, axis=-1) + w_prev * jnp.einsum('bhtd,bhd->bht', qc, n_vec)
        h = num / jnp.maximum(jnp.abs(den), jnp.exp(-m_t))[..., None]
        m_new = m_t[..., -1]
        w_s = jnp.exp(bcum[..., -1:] - bcum + igc - m_new[..., None])
        decay = jnp.exp(bcum[..., -1] + m - m_new)
        c_new = decay[..., None, None] * c_mat + jnp.einsum('bhs,bhsd,bhsv->bhdv', w_s, kc, vc)
        n_new = decay[..., None] * n_vec + jnp.einsum('bhs,bhsd->bhd', w_s, kc)
        return (c_new, n_new, m_new), h

    final, h = lax.scan(step, state, tuple(chunks(t) for t in (q, k, v, ig, lf)))
    h = jnp.moveaxis(h, 0, 2).reshape(b_sz, h_sz, t_len, dv)
    return h, final


def diff_attention(lat, ctx_p, qk_g, lam_p, out_g, lam_init, tabs, need_ctx):
    (ql, kl, vl), (qc, kc, vc) = lat, ctx_p

    def prep(t, g, rope):
        b_sz, t_len, _ = t.shape
        t = t.reshape(b_sz, t_len, DA_HEADS, 2, DA_HALF).transpose(0, 2, 3, 1, 4).astype(jnp.float32)
        t = rms_norm(t, g)
        return rope_2d(t, tabs) if rope else t

    lam = (jnp.exp(jnp.sum(lam_p[0] * lam_p[1])) - jnp.exp(jnp.sum(lam_p[2] * lam_p[3])) + lam_init).astype(jnp.float32)
    scale = DA_HALF ** -0.5

    def attend(qb, k, v):
        p = jax.nn.softmax(jnp.einsum('bhmqd,bhmkd->bhmqk', qb, k) * scale, axis=-1)
        return jnp.einsum('bhqk,bhkv->bhqv', p[:, :, 0] - lam * p[:, :, 1], v)

    k_ctx = prep(kc, qk_g[1], False)
    v_ctx = to_heads(vc, DA_HEADS).astype(jnp.float32)
    k_all = jnp.concatenate([k_ctx, prep(kl, qk_g[1], True)], axis=-2)
    v_all = jnp.concatenate([v_ctx, to_heads(vl, DA_HEADS).astype(jnp.float32)], axis=-2)
    o_lat = sweep_query_blocks(lambda qb: attend(qb, k_all, v_all), prep(ql, qk_g[0], True))

    def post(o):
        return from_heads(rms_norm(o, out_g) * (1.0 - lam_init))

    out_ctx = post(attend(prep(qc, qk_g[0], False), k_ctx, v_ctx)) if need_ctx else None
    return post(o_lat), out_ctx


def mlstm(lat, ctx_p, conv_w, conv_b, gate_b, out_g, need_ctx):
    hh = ML_HEADS

    def prep(q, k, v, o, g):
        qk = jax.nn.silu(dwconv_centred(jnp.concatenate([q, k], axis=-1), conv_w, conv_b))
        q, k = jnp.split(qk, 2, axis=-1)
        q = to_heads(q, hh).astype(jnp.float32)
        k = to_heads(k, hh).astype(jnp.float32) * ML_DK ** -0.5
        v = to_heads(v, hh).astype(jnp.float32)
        g = (g + gate_b).astype(jnp.float32).transpose(0, 2, 1)
        fwd = (g[:, 0:hh], jax.nn.log_sigmoid(g[:, hh:2 * hh]))
        bwd = (g[:, 2 * hh:3 * hh], jax.nn.log_sigmoid(g[:, 3 * hh:]))
        return (q, k, v), fwd, bwd, o

    lat_qkv, lat_f, lat_b, lat_o = prep(*lat)
    ctx_qkv, ctx_f, ctx_b, ctx_o = prep(*ctx_p)
    b_sz = lat_qkv[0].shape[0]
    s0 = (jnp.zeros((b_sz, hh, ML_DK, ML_DV), jnp.float32),
          jnp.zeros((b_sz, hh, ML_DK), jnp.float32),
          jnp.zeros((b_sz, hh), jnp.float32))

    def flip(ts):
        return tuple(jnp.flip(t, axis=2) for t in ts)

    h_cf, st_f = mlstm_chunkwise(*ctx_qkv, *ctx_f, s0)
    h_lf, _ = mlstm_chunkwise(*lat_qkv, *lat_f, st_f)
    h_cb, st_b = mlstm_chunkwise(*flip(ctx_qkv), *flip(ctx_b), s0)
    h_lb, _ = mlstm_chunkwise(*flip(lat_qkv), *flip(lat_b), st_b)
    gain = out_g.reshape(hh, 1, ML_DV)

    def post(h, o):
        return from_heads(rms_norm(h, gain)) * jax.nn.sigmoid(o.astype(jnp.float32))

    out_ctx = post(h_cf + jnp.flip(h_cb, axis=2), ctx_o) if need_ctx else None
    return post(h_lf + jnp.flip(h_lb, axis=2), lat_o), out_ctx


def mla(lat, ctx_p, q_norm_g, kv_norm_g, w_uq, w_ukv, qk_g, tabs, need_ctx):
    (cql, ckvl, kpel), (cqc, ckvc, kpec) = lat, ctx_p

    def rope_tail(t):
        return jnp.concatenate([t[..., :MLA_NOPE], rope_2d(t[..., MLA_NOPE:], tabs)], axis=-1)

    def queries(cq, rope):
        q = to_heads(rms_norm(cq, q_norm_g) @ w_uq, MLA_HEADS).astype(jnp.float32)
        q = rms_norm(q, qk_g[0])
        return rope_tail(q) if rope else q

    def keys_values(ckv, kpe, rope):
        b_sz, t_len, _ = ckv.shape
        kv = to_heads(rms_norm(ckv, kv_norm_g) @ w_ukv, MLA_HEADS).astype(jnp.float32)
        k_pe = jnp.broadcast_to(kpe.astype(jnp.float32)[:, None], (b_sz, MLA_HEADS, t_len, MLA_ROPE))
        k = rms_norm(jnp.concatenate([kv[..., :MLA_NOPE], k_pe], axis=-1), qk_g[1])
        return (rope_tail(k) if rope else k), kv[..., MLA_NOPE:]

    scale = MLA_QK ** -0.5

    def attend(qb, k, v):
        p = jax.nn.softmax(jnp.einsum('bhqd,bhkd->bhqk', qb, k) * scale, axis=-1)
        return jnp.einsum('bhqk,bhkv->bhqv', p, v)

    k_ctx, v_ctx = keys_values(ckvc, kpec, False)
    k_lat, v_lat = keys_values(ckvl, kpel, True)
    k_all = jnp.concatenate([k_ctx, k_lat], axis=-2)
    v_all = jnp.concatenate([v_ctx, v_lat], axis=-2)
    o_lat = sweep_query_blocks(lambda qb: attend(qb, k_all, v_all), queries(cql, True))
    out_ctx = from_heads(attend(queries(cqc, False), k_ctx, v_ctx)) if need_ctx else None
    return from_heads(o_lat), out_ctx


def hybrid_layer(x, xc, mod_lat, mod_ctx, layer_idx, need_ctx, tabs,
                 norm1_g, norm2_g, w_in, da_qk_g, da_lambda, da_out_g,
                 ml_conv_w, ml_conv_b, ml_gate_b, ml_out_g,
                 mla_q_norm_g, mla_kv_norm_g, mla_w_uq, mla_w_ukv, mla_qk_g,
                 w_out, ffn_w_gu, ffn_w_down):
    sh1, sc1, gt1, sh2, sc2, gt2 = jnp.split(mod_lat[:, None, :], 6, axis=-1)
    csh1, csc1, cgt1, csh2, csc2, cgt2 = jnp.split(mod_ctx, 6, axis=-1)
    pl = split_cols(modulate(x, norm1_g, sh1, sc1) @ w_in)
    pc = split_cols(modulate(xc, norm1_g, csh1, csc1) @ w_in)
    lam_init = 0.8 - 0.6 * math.exp(-0.3 * layer_idx)
    da_l, da_c = diff_attention(pl[0:3], pc[0:3], da_qk_g, da_lambda, da_out_g, lam_init, tabs, need_ctx)
    ml_l, ml_c = mlstm(pl[3:8], pc[3:8], ml_conv_w, ml_conv_b, ml_gate_b, ml_out_g, need_ctx)
    mla_l, mla_c = mla(pl[8:11], pc[8:11], mla_q_norm_g, mla_kv_norm_g, mla_w_uq, mla_w_ukv,
                       mla_qk_g, tabs, need_ctx)

    def finish(h, mix, g1, s2, c2, g2):
        h = h + g1 * (jnp.concatenate(mix, axis=-1).astype(h.dtype) @ w_out)
        return h + g2 * swiglu(modulate(h, norm2_g, s2, c2), ffn_w_gu, ffn_w_down)

    x = finish(x, [da_l, ml_l, mla_l], gt1, sh2, sc2, gt2)
    if need_ctx:
        xc = finish(xc, [da_c, ml_c, mla_c], cgt1, csh2, csc2, cgt2)
    return x, xc


def setup_inputs(seed: int = 0) -> dict:
    key = jax.random.key(seed)
    ks = jax.random.split(key, 32)
    f32 = jnp.float32

    def w(k, shape, fan_in):
        return jax.random.normal(k, shape, f32) * fan_in ** -0.5

    def gain(k, shape):
        return 1.0 + 0.05 * jax.random.normal(k, shape, f32)

    def small(k, shape, s=0.02):
        return s * jax.random.normal(k, shape, f32)

    ig_b = small(ks[14], (DEPTH, 2, ML_HEADS), 0.1)
    fg_b = jnp.linspace(3.0, 6.0, ML_HEADS, dtype=f32) + small(ks[15], (DEPTH, 2, ML_HEADS), 0.1)
    ml_gate_b = jnp.stack([ig_b, fg_b], axis=2).reshape(DEPTH, 4 * ML_HEADS)
    return {
        'x': jax.random.normal(ks[0], (BATCH, SEQ, D_MODEL), f32),
        'c': jax.random.normal(ks[1], (BATCH, D_MODEL), f32),
        'ctx': jax.random.normal(ks[2], (BATCH, CTX_LEN, D_MODEL), f32),
        'c_ctx': jax.random.normal(ks[3], (D_MODEL,), f32),
        'mod_w': w(ks[4], (DEPTH, D_MODEL, 6 * D_MODEL), D_MODEL),
        'mod_b': small(ks[5], (DEPTH, 6 * D_MODEL)),
        'norm1_g': gain(ks[6], (DEPTH, D_MODEL)),
        'norm2_g': gain(ks[7], (DEPTH, D_MODEL)),
        'w_in': w(ks[8], (DEPTH, D_MODEL, IN_W), D_MODEL),
        'da_qk_g': gain(ks[9], (DEPTH, 2, DA_HALF)),
        'da_lambda': small(ks[10], (DEPTH, 4, DA_HALF), 0.1),
        'da_out_g': gain(ks[11], (DEPTH, DA_V)),
        'ml_conv_w': w(ks[12], (DEPTH, ML_CONV, 2 * ML_HEADS * ML_DK), ML_CONV),
        'ml_conv_b': small(ks[13], (DEPTH, 2 * ML_HEADS * ML_DK)),
        'ml_gate_b': ml_gate_b,
        'ml_out_g': gain(ks[16], (DEPTH, ML_HEADS * ML_DV)),
        'mla_q_norm_g': gain(ks[17], (DEPTH, MLA_Q_RANK)),
        'mla_kv_norm_g': gain(ks[18], (DEPTH, MLA_KV_RANK)),
        'mla_w_uq': w(ks[19], (DEPTH, MLA_Q_RANK, MLA_HEADS * MLA_QK), MLA_Q_RANK),
        'mla_w_ukv': w(ks[20], (DEPTH, MLA_KV_RANK, MLA_HEADS * (MLA_NOPE + MLA_V)), MLA_KV_RANK),
        'mla_qk_g': gain(ks[21], (DEPTH, 2, MLA_QK)),
        'w_out': w(ks[22], (DEPTH, MIX_W, D_MODEL), MIX_W),
        'ffn_w_gu': w(ks[23], (DEPTH, D_MODEL, 2 * FF_HIDDEN), D_MODEL),
        'ffn_w_down': w(ks[24], (DEPTH, FF_HIDDEN, D_MODEL), FF_HIDDEN),
    }


def reference(x, c, ctx, c_ctx, mod_w, mod_b, norm1_g, norm2_g, w_in, da_qk_g, da_lambda,
              da_out_g, ml_conv_w, ml_conv_b, ml_gate_b, ml_out_g, mla_q_norm_g, mla_kv_norm_g,
              mla_w_uq, mla_w_ukv, mla_qk_g, w_out, ffn_w_gu, ffn_w_down):
    in_dtype = x.dtype
    tabs = rope_tables(x.shape[1])
    xc = ctx
    for l in range(DEPTH):
        mod_lat = jax.nn.silu(c) @ mod_w[l] + mod_b[l]
        mod_ctx = jax.nn.silu(c_ctx) @ mod_w[l] + mod_b[l]
        x, xc = hybrid_layer(
            x, xc, mod_lat, mod_ctx, l, l < DEPTH - 1, tabs,
            norm1_g[l], norm2_g[l], w_in[l], da_qk_g[l], da_lambda[l], da_out_g[l],
            ml_conv_w[l], ml_conv_b[l], ml_gate_b[l], ml_out_g[l],
            mla_q_norm_g[l], mla_kv_norm_g[l], mla_w_uq[l], mla_w_ukv[l], mla_qk_g[l],
            w_out[l], ffn_w_gu[l], ffn_w_down[l])
    return x.astype(in_dtype)
```

```python
import functools
import math

import jax
import jax.numpy as jnp
from jax import lax
from jax.experimental import pallas as pl
from jax.experimental.pallas import tpu as pltpu

F32 = jnp.float32
BF16 = jnp.bfloat16

DA_HEADS = 4
DA_HALF = 64
DA_V = 2 * DA_HALF
ML_HEADS = 4
ML_DK = 128
ML_DV = 128
MLA_HEADS = 8
MLA_NOPE = 128
MLA_ROPE = 64
MLA_V = 128
MLA_QK = MLA_NOPE + MLA_ROPE
MLA_PAD = 256
GRID_W = 64
ROPE_DIM = 64
ROPE_BASE = 10000.0
EPS = 1e-6
LOG2E = 1.4426950408889634

LANES = 128
ML_CHUNK = 128
VMEM_LIMIT = 56 * 1024 * 1024

C_DAQ, C_DAK, C_DAV = 0, 512, 1024
C_MLQ, C_MLK, C_MLV, C_MLO = 1536, 2048, 2560, 3072
C_CQ, C_CKV, C_KPE, C_GATE = 3584, 4096, 4352, 4480
P_COLS = 4608


def _cparams(*sem):
    return pltpu.CompilerParams(dimension_semantics=sem, vmem_limit_bytes=VMEM_LIMIT)


def _pick(n, pref):
    if n <= pref:
        return n
    t = pref
    while n % t:
        t //= 2
    return t


def _mod_kernel(c_ref, w_ref, b_ref, o_ref):
    c = c_ref[...]
    s = (c / (1.0 + jnp.exp(-c))).astype(BF16)
    o_ref[...] = jnp.dot(s, w_ref[...].astype(BF16), preferred_element_type=F32) + b_ref[...]


def _modulation(cc, mod_w, mod_b):
    depth, d, n = mod_w.shape
    tn = _pick(n, 1024)
    return pl.pallas_call(
        _mod_kernel,
        out_shape=jax.ShapeDtypeStruct((depth, cc.shape[0], n), F32),
        grid=(depth, n // tn),
        in_specs=[
            pl.BlockSpec((cc.shape[0], d), lambda l, j: (0, 0)),
            pl.BlockSpec((None, d, tn), lambda l, j: (l, 0, j)),
            pl.BlockSpec((None, 1, tn), lambda l, j: (l, 0, j)),
        ],
        out_specs=pl.BlockSpec((None, cc.shape[0], tn), lambda l, j: (l, 0, j)),
        compiler_params=_cparams("arbitrary", "arbitrary"),
        name="modulation",
    )(cc, mod_w, mod_b.reshape(depth, 1, n))


def _norm_rows(x_ref, g_ref, shift, scale1, out_ref, rc):
    tm = x_ref.shape[0]
    g = g_ref[...]

    def body(r, carry):
        rows = pl.ds(pl.multiple_of(r * rc, rc), rc)
        x = x_ref[rows, :]
        ms = jnp.mean(x * x, axis=-1, keepdims=True)
        y = x * lax.rsqrt(ms + EPS) * g
        out_ref[rows, :] = (y * scale1 + shift).astype(out_ref.dtype)
        return carry

    lax.fori_loop(0, tm // rc, body, 0)


def _in_proj_kernel(x_ref, mod_ref, g_ref, w_ref, p_ref, gate_ref, xn_ref, *, d, rc):
    n = pl.program_id(1)

    @pl.when(n == 0)
    def _():
        shift = mod_ref[:, 0:d]
        scale1 = 1.0 + mod_ref[:, d:2 * d]
        _norm_rows(x_ref, g_ref, shift, scale1, xn_ref, rc)

    acc = jnp.dot(xn_ref[...], w_ref[...], preferred_element_type=F32)
    p_ref[...] = acc.astype(p_ref.dtype)

    @pl.when(n == pl.num_programs(1) - 1)
    def _():
        gate_ref[...] = acc[:, acc.shape[1] - LANES:]


def _in_proj(x, mod, g, w, rows_per_mod, mod_base):
    r, d = x.shape
    n = w.shape[1]
    tm = _pick(min(r, rows_per_mod), 1024)
    tn = 768
    assert n % tn == 0
    return pl.pallas_call(
        functools.partial(_in_proj_kernel, d=d, rc=_pick(tm, 64)),
        out_shape=(jax.ShapeDtypeStruct((r, n), BF16), jax.ShapeDtypeStruct((r, LANES), F32)),
        grid=(r // tm, n // tn),
        in_specs=[
            pl.BlockSpec((tm, d), lambda i, j: (i, 0)),
            pl.BlockSpec((None, 1, 6 * d), lambda i, j: (i * tm // rows_per_mod + mod_base, 0, 0)),
            pl.BlockSpec((1, d), lambda i, j: (0, 0)),
            pl.BlockSpec((d, tn), lambda i, j: (0, j)),
        ],
        out_specs=(
            pl.BlockSpec((tm, tn), lambda i, j: (i, j)),
            pl.BlockSpec((tm, LANES), lambda i, j: (i, 0)),
        ),
        scratch_shapes=[pltpu.VMEM((tm, d), BF16)],
        compiler_params=_cparams("arbitrary", "arbitrary"),
        name="in_proj",
    )(x, mod, g, w)


def _swap16(x):
    lane = lax.broadcasted_iota(jnp.int32, x.shape, 1)
    fwd = pltpu.roll(x, LANES - 16, 1)
    bwd = pltpu.roll(x, 16, 1)
    return jnp.where((lane % 32) < 16, fwd, bwd)


def _rope(x, cos, sin):
    return x * cos + _swap16(x) * sin


def _group_mean_sq(x, gm):
    x2 = x * x
    hi = x2.astype(BF16)
    r1 = x2 - hi.astype(F32)
    mid = r1.astype(BF16)
    lo = (r1 - mid.astype(F32)).astype(BF16)
    return (jnp.dot(hi, gm, preferred_element_type=F32)
            + jnp.dot(mid, gm, preferred_element_type=F32)
            + jnp.dot(lo, gm, preferred_element_type=F32))


def _da_prep_kernel(q_ref, k_ref, cos_ref, sin_ref, gq_ref, gk_ref, gm_ref, qo_ref, ko_ref, *, qscale):
    cos = cos_ref[...]
    sin = sin_ref[...]
    gm = gm_ref[...]
    lane = lax.broadcasted_iota(jnp.int32, cos.shape, 1)
    first = lane < DA_HALF
    for h in range(DA_HEADS):
        sl = slice(h * LANES, (h + 1) * LANES)
        q = q_ref[:, sl].astype(F32)
        q = q * lax.rsqrt(_group_mean_sq(q, gm) + EPS) * gq_ref[...]
        q = _rope(q, cos, sin) * qscale
        qo_ref[:, 2 * h * LANES:(2 * h + 1) * LANES] = jnp.where(first, q, 0.0).astype(qo_ref.dtype)
        qo_ref[:, (2 * h + 1) * LANES:(2 * h + 2) * LANES] = jnp.where(first, 0.0, q).astype(qo_ref.dtype)
        k = k_ref[:, sl].astype(F32)
        k = k * lax.rsqrt(_group_mean_sq(k, gm) + EPS) * gk_ref[...]
        ko_ref[:, sl] = _rope(k, cos, sin).astype(ko_ref.dtype)


def _da_prep(p, cos, sin, gq, gk, gm):
    r = p.shape[0]
    tm = _pick(r, 256)
    tab_blocks = cos.shape[0] // tm
    w = DA_HEADS * LANES
    qscale = DA_HALF ** -0.5 * LOG2E
    return pl.pallas_call(
        functools.partial(_da_prep_kernel, qscale=qscale),
        out_shape=(jax.ShapeDtypeStruct((r, 2 * w), BF16), jax.ShapeDtypeStruct((r, w), BF16)),
        grid=(r // tm,),
        in_specs=[
            pl.BlockSpec((tm, w), lambda i: (i, C_DAQ // w)),
            pl.BlockSpec((tm, w), lambda i: (i, C_DAK // w)),
            pl.BlockSpec((tm, LANES), lambda i: (i % tab_blocks, 0)),
            pl.BlockSpec((tm, LANES), lambda i: (i % tab_blocks, 0)),
            pl.BlockSpec((1, LANES), lambda i: (0, 0)),
            pl.BlockSpec((1, LANES), lambda i: (0, 0)),
            pl.BlockSpec((LANES, LANES), lambda i: (0, 0)),
        ],
        out_specs=(
            pl.BlockSpec((tm, 2 * w), lambda i: (i, 0)),
            pl.BlockSpec((tm, w), lambda i: (i, 0)),
        ),
        compiler_params=_cparams("arbitrary"),
        name="da_prep",
    )(p, p, cos, sin, gq, gk, gm)


def _mla_prep_kernel(cq_ref, ckv_ref, kpe_ref, cos_ref, sin_ref, gcq_ref, gckv_ref, gq_ref, gk_ref,
                     wq_ref, wkv_ref, qo_ref, ko_ref, vo_ref, *, qscale):
    cos = cos_ref[...]
    sin = sin_ref[...]
    lane = lax.broadcasted_iota(jnp.int32, cos.shape, 1)
    inv_qk = 1.0 / MLA_QK

    cq = cq_ref[...].astype(F32)
    cq = cq * lax.rsqrt(jnp.mean(cq * cq, axis=-1, keepdims=True) + EPS) * gcq_ref[...]
    qf = jnp.dot(cq.astype(BF16), wq_ref[...], preferred_element_type=F32)

    ckv = ckv_ref[...].astype(F32)
    ckv = ckv * lax.rsqrt(jnp.mean(ckv * ckv, axis=-1, keepdims=True) + EPS) * gckv_ref[...]
    kvf = jnp.dot(ckv.astype(BF16), wkv_ref[...], preferred_element_type=F32)

    kpe = jnp.where(lane < MLA_ROPE, kpe_ref[...].astype(F32), 0.0)
    kpe_ss = jnp.sum(kpe * kpe, axis=-1, keepdims=True)
    kpe_rot = _rope(kpe * gk_ref[:, LANES:], cos, sin)

    for h in range(MLA_HEADS):
        q1 = qf[:, h * MLA_PAD:h * MLA_PAD + LANES]
        q2 = qf[:, h * MLA_PAD + LANES:(h + 1) * MLA_PAD]
        ms = (jnp.sum(q1 * q1, axis=-1, keepdims=True) + jnp.sum(q2 * q2, axis=-1, keepdims=True)) * inv_qk
        rq = lax.rsqrt(ms + EPS) * qscale
        qo_ref[:, h * MLA_PAD:h * MLA_PAD + LANES] = (q1 * rq * gq_ref[:, :LANES]).astype(qo_ref.dtype)
        q2 = _rope(q2 * rq * gq_ref[:, LANES:], cos, sin)
        qo_ref[:, h * MLA_PAD + LANES:(h + 1) * MLA_PAD] = q2.astype(qo_ref.dtype)

        kn = kvf[:, h * MLA_PAD:h * MLA_PAD + LANES]
        ms = (jnp.sum(kn * kn, axis=-1, keepdims=True) + kpe_ss) * inv_qk
        rk = lax.rsqrt(ms + EPS)
        ko_ref[:, h * MLA_PAD:h * MLA_PAD + LANES] = (kn * rk * gk_ref[:, :LANES]).astype(ko_ref.dtype)
        ko_ref[:, h * MLA_PAD + LANES:(h + 1) * MLA_PAD] = (kpe_rot * rk).astype(ko_ref.dtype)
        vo_ref[:, h * MLA_V:(h + 1) * MLA_V] = kvf[:, h * MLA_PAD + LANES:(h + 1) * MLA_PAD].astype(vo_ref.dtype)


def _mla_prep(p, cos, sin, gcq, gckv, gq, gk, wq, wkv):
    r = p.shape[0]
    tm = _pick(r, 256)
    tab_blocks = cos.shape[0] // tm
    q_rank = wq.shape[0]
    kv_rank = wkv.shape[0]
    wide = MLA_HEADS * MLA_PAD
    qscale = MLA_QK ** -0.5 * LOG2E
    const = lambda i: (0, 0)
    return pl.pallas_call(
        functools.partial(_mla_prep_kernel, qscale=qscale),
        out_shape=(jax.ShapeDtypeStruct((r, wide), BF16), jax.ShapeDtypeStruct((r, wide), BF16),
                   jax.ShapeDtypeStruct((r, MLA_HEADS * MLA_V), BF16)),
        grid=(r // tm,),
        in_specs=[
            pl.BlockSpec((tm, q_rank), lambda i: (i, C_CQ // q_rank)),
            pl.BlockSpec((tm, kv_rank), lambda i: (i, C_CKV // kv_rank)),
            pl.BlockSpec((tm, LANES), lambda i: (i, C_KPE // LANES)),
            pl.BlockSpec((tm, LANES), lambda i: (i % tab_blocks, 0)),
            pl.BlockSpec((tm, LANES), lambda i: (i % tab_blocks, 0)),
            pl.BlockSpec((1, q_rank), const),
            pl.BlockSpec((1, kv_rank), const),
            pl.BlockSpec((1, MLA_PAD), const),
            pl.BlockSpec((1, MLA_PAD), const),
            pl.BlockSpec((q_rank, wide), const),
            pl.BlockSpec((kv_rank, wide), const),
        ],
        out_specs=(
            pl.BlockSpec((tm, wide), lambda i: (i, 0)),
            pl.BlockSpec((tm, wide), lambda i: (i, 0)),
            pl.BlockSpec((tm, MLA_HEADS * MLA_V), lambda i: (i, 0)),
        ),
        compiler_params=_cparams("arbitrary"),
        name="mla_prep",
    )(p, p, p, cos, sin, gcq, gckv, gq, gk, wq, wkv)


def _attn_core(q_ref, k_refs, v_refs, s_ref, n_comp, dq, kc):
    chunks = []
    col = 0
    for ri, kr in enumerate(k_refs):
        rows = kr.shape[0]
        step = min(kc, rows)
        for r0 in range(0, rows, step):
            chunks.append((ri, r0, step, col))
            col += step
    outs = []
    for c in range(n_comp):
        q = q_ref[:, c * dq:(c + 1) * dq]
        m = None
        for ri, r0, step, c0 in chunks:
            s = lax.dot_general(q, k_refs[ri][r0:r0 + step, :], (((1,), (1,)), ((), ())),
                                preferred_element_type=F32)
            s_ref[:, c0:c0 + step] = s
            cm = jnp.max(s, axis=-1, keepdims=True)
            m = cm if m is None else jnp.maximum(m, cm)
        l = None
        acc = None
        for ri, r0, step, c0 in chunks:
            p = jnp.exp2(s_ref[:, c0:c0 + step] - m)
            ps = jnp.sum(p, axis=-1, keepdims=True)
            pv = jnp.dot(p.astype(BF16), v_refs[ri][r0:r0 + step, :], preferred_element_type=F32)
            l = ps if l is None else l + ps
            acc = pv if acc is None else acc + pv
        outs.append(acc / l)
    return outs


def _da_attn_kernel(*refs, n_kv, lam_init, kc):
    q_ref = refs[0]
    k_refs = refs[1:1 + n_kv]
    v_refs = refs[1 + n_kv:1 + 2 * n_kv]
    lam_ref, g_ref, o_ref, s_ref = refs[1 + 2 * n_kv:]
    lp = lam_ref[...]
    lam = (jnp.exp(jnp.sum(lp[0:1] * lp[1:2], axis=-1, keepdims=True))
           - jnp.exp(jnp.sum(lp[2:3] * lp[3:4], axis=-1, keepdims=True)) + lam_init)
    o0, o1 = _attn_core(q_ref, k_refs, v_refs, s_ref, 2, LANES, kc)
    o = o0 - lam * o1
    o = o * lax.rsqrt(jnp.mean(o * o, axis=-1, keepdims=True) + EPS) * g_ref[...]
    o_ref[...] = (o * (1.0 - lam_init)).astype(o_ref.dtype)


def _mla_attn_kernel(*refs, n_kv, kc):
    q_ref = refs[0]
    k_refs = refs[1:1 + n_kv]
    v_refs = refs[1 + n_kv:1 + 2 * n_kv]
    o_ref, s_ref = refs[1 + 2 * n_kv:]
    (o,) = _attn_core(q_ref, k_refs, v_refs, s_ref, 1, MLA_PAD, kc)
    o_ref[...] = o.astype(o_ref.dtype)


def _attention(kind, q, kvs, batch, extra, lam_init=0.0):
    heads = DA_HEADS if kind == "da" else MLA_HEADS
    qw = 2 * LANES if kind == "da" else MLA_PAD
    dq = LANES if kind == "da" else MLA_PAD
    dv = DA_V if kind == "da" else MLA_V
    rq = q.shape[0]
    tq = _pick(rq // batch, 512)
    nq = rq // batch // tq
    kc = 1024
    k_specs, v_specs, k_args, v_args = [], [], [], []
    total_keys = 0
    for k, v, vcol in kvs:
        kl = k.shape[0] // batch
        total_keys += kl
        k_specs.append(pl.BlockSpec((kl, dq), lambda b, h, i: (b, h)))
        v_specs.append(pl.BlockSpec((kl, dv), lambda b, h, i, vcol=vcol: (b, vcol // dv + h)))
        k_args.append(k)
        v_args.append(v)
    extra_specs = [pl.BlockSpec(e.shape, lambda b, h, i: (0, 0)) for e in extra]
    if kind == "da":
        body = functools.partial(_da_attn_kernel, n_kv=len(kvs), lam_init=lam_init, kc=kc)
    else:
        body = functools.partial(_mla_attn_kernel, n_kv=len(kvs), kc=kc)
    return pl.pallas_call(
        body,
        out_shape=jax.ShapeDtypeStruct((rq, heads * dv), BF16),
        grid=(batch, heads, nq),
        in_specs=[pl.BlockSpec((tq, qw), lambda b, h, i: (b * nq + i, h))] + k_specs + v_specs + extra_specs,
        out_specs=pl.BlockSpec((tq, dv), lambda b, h, i: (b * nq + i, h)),
        scratch_shapes=[pltpu.VMEM((tq, total_keys), F32)],
        compiler_params=_cparams("arbitrary", "arbitrary", "arbitrary"),
        name=kind + "_attn",
    )(q, *k_args, *v_args, *extra)


def _ml_conv_kernel(x_ref, w_ref, b_ref, o_ref, *, kscale, n_q_blocks):
    j = pl.program_id(1)
    x = x_ref[...].astype(F32)
    s = x.shape[0]
    t = lax.broadcasted_iota(jnp.int32, x.shape, 0)
    prev = jnp.where(t == 0, 0.0, pltpu.roll(x, 1, 0))
    nxt = jnp.where(t == s - 1, 0.0, pltpu.roll(x, s - 1, 0))
    y = b_ref[...] + prev * w_ref[0:1, :] + x * w_ref[1:2, :] + nxt * w_ref[2:3, :]
    y = y / (1.0 + jnp.exp(-y))
    scale = jnp.where(j >= n_q_blocks, kscale, 1.0)
    o_ref[...] = (y * scale).astype(o_ref.dtype)


def _ml_conv(p3, conv_w, conv_b):
    nseq, s, _ = p3.shape
    nblk = 2 * ML_HEADS * ML_DK // LANES
    return pl.pallas_call(
        functools.partial(_ml_conv_kernel, kscale=ML_DK ** -0.5, n_q_blocks=nblk // 2),
        out_shape=jax.ShapeDtypeStruct((nseq, s, nblk * LANES), BF16),
        grid=(nseq, nblk),
        in_specs=[
            pl.BlockSpec((None, s, LANES), lambda b, j: (b, 0, C_MLQ // LANES + j)),
            pl.BlockSpec((3, LANES), lambda b, j: (0, j)),
            pl.BlockSpec((1, LANES), lambda b, j: (0, j)),
        ],
        out_specs=pl.BlockSpec((None, s, LANES), lambda b, j: (b, 0, j)),
        compiler_params=_cparams("arbitrary", "arbitrary"),
        name="ml_conv",
    )(p3, conv_w, conv_b)


def _ml_gate_kernel(g_ref, b_ref, gt_ref, gg_ref, *, rc):
    s = g_ref.shape[0]
    for r0 in range(0, s, rc):
        g = g_ref[r0:r0 + rc, :] + b_ref[...]
        lane = lax.broadcasted_iota(jnp.int32, g.shape, 1)
        is_forget = ((lane % (2 * ML_HEADS)) >= ML_HEADS) & (lane < 4 * ML_HEADS)
        lf = jnp.minimum(g, 0.0) - jnp.log(1.0 + jnp.exp(-jnp.abs(g)))
        out = jnp.where(is_forget, lf, g)
        gt_ref[r0:r0 + rc, :] = out
        gg_ref[:, r0:r0 + rc] = out.T[0:4 * ML_HEADS, :]


def _ml_gates(g3, gate_b):
    nseq, s, _ = g3.shape
    return pl.pallas_call(
        functools.partial(_ml_gate_kernel, rc=_pick(s, 512)),
        out_shape=(jax.ShapeDtypeStruct((nseq, s, LANES), F32),
                   jax.ShapeDtypeStruct((nseq, 4 * ML_HEADS, s), F32)),
        grid=(nseq,),
        in_specs=[
            pl.BlockSpec((None, s, LANES), lambda b: (b, 0, 0)),
            pl.BlockSpec((1, LANES), lambda b: (0, 0)),
        ],
        out_specs=(
            pl.BlockSpec((None, s, LANES), lambda b: (b, 0, 0)),
            pl.BlockSpec((None, 4 * ML_HEADS, s), lambda b: (b, 0, 0)),
        ),
        compiler_params=_cparams("arbitrary"),
        name="ml_gates",
    )(g3, gate_b)


def _split3(x):
    hi = x.astype(BF16)
    r1 = x - hi.astype(F32)
    mid = r1.astype(BF16)
    lo = (r1 - mid.astype(F32)).astype(BF16)
    return hi, mid, lo


def _ml_scan_kernel(*refs, nb, reverse, post):
    if post:
        (q_ref, k_ref, v_ref, gt_ref, gg_ref, c0_ref, m0_ref, hp_ref, og_ref, gain_ref,
         h_ref, c_out_ref, m_out_ref, c_sc, m_sc) = refs
    else:
        (q_ref, k_ref, v_ref, gt_ref, gg_ref, c0_ref, m0_ref,
         h_ref, c_out_ref, m_out_ref, c_sc, m_sc) = refs
    step = pl.program_id(0)
    L = q_ref.shape[1]
    nh = ML_HEADS

    @pl.when(step == 0)
    def _():
        c_sc[...] = c0_ref[...]
        m_sc[...] = m0_ref[...]

    row = lax.broadcasted_iota(jnp.int32, (L, L), 0)
    col = lax.broadcasted_iota(jnp.int32, (L, L), 1)
    incl = (col >= row) if reverse else (col <= row)
    incl_b = jnp.where(incl, 1.0, 0.0).astype(BF16)
    incl_t = jnp.where((row >= col) if reverse else (row <= col), 1.0, 0.0).astype(BF16)
    last = 0 if reverse else L - 1
    ones = jnp.ones((L, ML_DV), BF16)

    for b in range(nb):
        gt = gt_ref[b]
        gg = gg_ref[b]
        bc_col = sum(jnp.dot(incl_b, part, preferred_element_type=F32) for part in _split3(gt))
        bc_row = sum(jnp.dot(part, incl_t, preferred_element_type=F32) for part in _split3(gg))
        for h in range(nh):
            bh = b * nh + h
            ig_lane = (2 * nh if reverse else 0) + h
            lf_lane = ig_lane + nh
            hs = slice(h * ML_DK, (h + 1) * ML_DK)
            q = q_ref[b, :, hs]
            k = k_ref[b, :, hs]
            v = v_ref[b, :, hs]
            c_aug = c_sc[bh]
            m_prev = m_sc[bh][0:1, 0:1]

            bcum_col = bc_col[:, lf_lane:lf_lane + 1]
            bcum_row = bc_row[lf_lane:lf_lane + 1, :]
            ig_col = gt[:, ig_lane:ig_lane + 1]
            ig_row = gg[ig_lane:ig_lane + 1, :]
            total = bcum_col[last:last + 1, :]

            log_d = jnp.where(incl, bcum_col - bcum_row + ig_row, -jnp.inf)
            log_prev = bcum_col + m_prev
            m_t = jnp.maximum(log_prev, jnp.max(log_d, axis=-1, keepdims=True))
            dmat = jnp.exp(log_d - m_t)
            w_prev = jnp.exp(log_prev - m_t)
            s = lax.dot_general(q, k, (((1,), (1,)), ((), ())), preferred_element_type=F32) * dmat
            v_aug = jnp.concatenate([v, ones], axis=-1)
            tot = (jnp.dot(s.astype(BF16), v_aug, preferred_element_type=F32)
                   + w_prev * jnp.dot(q, c_aug.astype(BF16), preferred_element_type=F32))
            num = tot[:, :ML_DV]
            den = tot[:, ML_DV:]
            hv = num / jnp.maximum(jnp.abs(den), jnp.exp(-m_t))

            m_new = m_t[last:last + 1, :]
            w_s = jnp.exp(total - bcum_col + ig_col - m_new)
            decay = jnp.exp(total + m_prev - m_new)
            kv = lax.dot_general(k, (w_s * v_aug.astype(F32)).astype(BF16), (((0,), (0,)), ((), ())),
                                 preferred_element_type=F32)
            c_sc[bh] = decay * c_aug + kv
            m_sc[bh] = jnp.broadcast_to(m_new, m_sc.shape[1:])

            if post:
                hv = hv + hp_ref[b, :, hs]
                hv = hv * lax.rsqrt(jnp.mean(hv * hv, axis=-1, keepdims=True) + EPS) * gain_ref[:, hs]
                og = og_ref[b, :, hs].astype(F32)
                hv = hv / (1.0 + jnp.exp(-og))
            h_ref[b, :, hs] = hv.astype(h_ref.dtype)

    @pl.when(step == pl.num_programs(0) - 1)
    def _():
        c_out_ref[...] = c_sc[...]
        m_out_ref[...] = m_sc[...]


def _ml_scan(qk3, p3, gt3, gg3, state, reverse, post_args=None):
    nb, s, _ = qk3.shape
    L = min(ML_CHUNK, s)
    nc = s // L
    w = ML_HEADS * ML_DK
    c0, m0 = state
    post = post_args is not None
    cidx = (lambda c: nc - 1 - c) if reverse else (lambda c: c)
    in_specs = [
        pl.BlockSpec((nb, L, w), lambda c: (0, cidx(c), 0)),
        pl.BlockSpec((nb, L, w), lambda c: (0, cidx(c), 1)),
        pl.BlockSpec((nb, L, w), lambda c: (0, cidx(c), C_MLV // w)),
        pl.BlockSpec((nb, L, LANES), lambda c: (0, cidx(c), 0)),
        pl.BlockSpec((nb, 4 * ML_HEADS, L), lambda c: (0, 0, cidx(c))),
        pl.BlockSpec(c0.shape, lambda c: (0, 0, 0)),
        pl.BlockSpec(m0.shape, lambda c: (0, 0, 0)),
    ]
    args = [qk3, qk3, p3, gt3, gg3, c0, m0]
    if post:
        h_prev, gain = post_args
        in_specs += [
            pl.BlockSpec((nb, L, w), lambda c: (0, cidx(c), 0)),
            pl.BlockSpec((nb, L, w), lambda c: (0, cidx(c), C_MLO // w)),
            pl.BlockSpec((1, w), lambda c: (0, 0)),
        ]
        args += [h_prev, p3, gain]
    return pl.pallas_call(
        functools.partial(_ml_scan_kernel, nb=nb, reverse=reverse, post=post),
        out_shape=(jax.ShapeDtypeStruct((nb, s, w), BF16 if post else F32),
                   jax.ShapeDtypeStruct(c0.shape, F32), jax.ShapeDtypeStruct(m0.shape, F32)),
        grid=(nc,),
        in_specs=in_specs,
        out_specs=(
            pl.BlockSpec((nb, L, w), lambda c: (0, cidx(c), 0)),
            pl.BlockSpec(c0.shape, lambda c: (0, 0, 0)),
            pl.BlockSpec(m0.shape, lambda c: (0, 0, 0)),
        ),
        scratch_shapes=[pltpu.VMEM(c0.shape, F32), pltpu.VMEM(m0.shape, F32)],
        compiler_params=_cparams("arbitrary"),
        name="ml_scan",
    )(*args)


def _out_proj_kernel(da_ref, ml_ref, mla_ref, w_ref, x_ref, gate_ref, h_ref):
    n_da = da_ref.shape[1]
    n_ml = ml_ref.shape[1]
    acc = jnp.dot(da_ref[...], w_ref[0:n_da, :], preferred_element_type=F32)
    acc += jnp.dot(ml_ref[...], w_ref[n_da:n_da + n_ml, :], preferred_element_type=F32)
    acc += jnp.dot(mla_ref[...], w_ref[n_da + n_ml:, :], preferred_element_type=F32)
    h_ref[...] = x_ref[...] + gate_ref[...] * acc


def _out_proj(da, ml, mla, w, x, mod, rows_per_mod, mod_base):
    r, d = x.shape
    tm = _pick(min(r, rows_per_mod), 1024)
    tn = _pick(d, 1024)
    nt = d // tn
    return pl.pallas_call(
        _out_proj_kernel,
        out_shape=jax.ShapeDtypeStruct((r, d), F32),
        grid=(nt, r // tm),
        in_specs=[
            pl.BlockSpec((tm, da.shape[1]), lambda j, i: (i, 0)),
            pl.BlockSpec((tm, ml.shape[1]), lambda j, i: (i, 0)),
            pl.BlockSpec((tm, mla.shape[1]), lambda j, i: (i, 0)),
            pl.BlockSpec((w.shape[0], tn), lambda j, i: (0, j)),
            pl.BlockSpec((tm, tn), lambda j, i: (i, j)),
            pl.BlockSpec((None, 1, tn), lambda j, i: (i * tm // rows_per_mod + mod_base, 0, 2 * nt + j)),
        ],
        out_specs=pl.BlockSpec((tm, tn), lambda j, i: (i, j)),
        compiler_params=_cparams("arbitrary", "arbitrary"),
        name="out_proj",
    )(da, ml, mla, w, x, mod)


def _ffn_kernel(h_ref, mod_ref, g_ref, wg_ref, wu_ref, wd_ref, o_ref, hn_ref, *, d, rc):
    j = pl.program_id(1)

    @pl.when(j == 0)
    def _():
        shift = mod_ref[:, 3 * d:4 * d]
        scale1 = 1.0 + mod_ref[:, 4 * d:5 * d]
        _norm_rows(h_ref, g_ref, shift, scale1, hn_ref, rc)

    hn = hn_ref[...]
    gate = jnp.dot(hn, wg_ref[...], preferred_element_type=F32)
    up = jnp.dot(hn, wu_ref[...], preferred_element_type=F32)
    act = (gate / (1.0 + jnp.exp(-gate)) * up).astype(BF16)
    part = mod_ref[:, 5 * d:6 * d] * jnp.dot(act, wd_ref[...], preferred_element_type=F32)

    @pl.when(j == 0)
    def _():
        o_ref[...] = h_ref[...] + part

    @pl.when(j > 0)
    def _():
        o_ref[...] += part


def _ffn(h, mod, g, w_gu, w_down, rows_per_mod, mod_base):
    r, d = h.shape
    ff = w_down.shape[0]
    tm = _pick(min(r, rows_per_mod), 512)
    th = 512 if ff % 512 == 0 else 256
    nh = ff // th
    return pl.pallas_call(
        functools.partial(_ffn_kernel, d=d, rc=_pick(tm, 64)),
        out_shape=jax.ShapeDtypeStruct((r, d), F32),
        grid=(r // tm, nh),
        in_specs=[
            pl.BlockSpec((tm, d), lambda i, j: (i, 0)),
            pl.BlockSpec((None, 1, 6 * d), lambda i, j: (i * tm // rows_per_mod + mod_base, 0, 0)),
            pl.BlockSpec((1, d), lambda i, j: (0, 0)),
            pl.BlockSpec((d, th), lambda i, j: (0, j)),
            pl.BlockSpec((d, th), lambda i, j: (0, nh + j)),
            pl.BlockSpec((th, d), lambda i, j: (j, 0)),
        ],
        out_specs=pl.BlockSpec((tm, d), lambda i, j: (i, 0)),
        scratch_shapes=[pltpu.VMEM((tm, d), BF16)],
        compiler_params=_cparams("arbitrary", "arbitrary"),
        name="ffn",
    )(h, mod, g, w_gu, w_gu, w_down)


def _rope_tables(n_lat):
    pos = jnp.arange(n_lat, dtype=jnp.int32)
    r = (pos // GRID_W).astype(F32)
    c = (pos % GRID_W).astype(F32)
    half = ROPE_DIM // 4
    inv = ROPE_BASE ** (-jnp.arange(half, dtype=F32) / half)
    ar = r[:, None] * inv
    ac = c[:, None] * inv
    cos64 = jnp.concatenate([jnp.cos(ar), jnp.cos(ar), jnp.cos(ac), jnp.cos(ac)], axis=-1)
    sin64 = jnp.concatenate([-jnp.sin(ar), jnp.sin(ar), -jnp.sin(ac), jnp.sin(ac)], axis=-1)
    return cos64, sin64


def _pad_cols(a, width):
    return jnp.pad(a, ((0, 0), (0, width - a.shape[1])))


def _layout_w_in(w):
    gates0 = C_CQ
    gates1 = gates0 + 4 * ML_HEADS
    main = w[:, :gates0]
    gates = w[:, gates0:gates1]
    rest = w[:, gates1:]
    n_cq_ckv = C_KPE - C_CQ
    cq_ckv = rest[:, :n_cq_ckv]
    kpe = rest[:, n_cq_ckv:]
    return jnp.concatenate([main, cq_ckv, _pad_cols(kpe, LANES), _pad_cols(gates, LANES)], axis=1)


def kernel(x, c, ctx, c_ctx, mod_w, mod_b, norm1_g, norm2_g, w_in, da_qk_g, da_lambda, da_out_g,
           ml_conv_w, ml_conv_b, ml_gate_b, ml_out_g, mla_q_norm_g, mla_kv_norm_g, mla_w_uq, mla_w_ukv,
           mla_qk_g, w_out, ffn_w_gu, ffn_w_down):
    batch, seq, d = x.shape
    ctx_len = ctx.shape[1]
    depth = mod_w.shape[0]
    in_dtype = x.dtype
    assert w_in.shape[2] == C_CQ + 4 * ML_HEADS + (C_KPE - C_CQ) + MLA_ROPE

    n_mod_rows = -(-(batch + 1) // 8) * 8
    cc = jnp.concatenate([c, c_ctx[None, :], jnp.zeros((n_mod_rows - batch - 1, d), F32)], axis=0)
    mod_all = _modulation(cc, mod_w, mod_b)

    cos64, sin64 = _rope_tables(seq)
    cos_da = jnp.tile(cos64, (1, 2))
    sin_da = jnp.tile(sin64, (1, 2))
    cos_mla = jnp.concatenate([cos64, jnp.ones_like(cos64)], axis=-1)
    sin_mla = jnp.concatenate([sin64, jnp.zeros_like(sin64)], axis=-1)
    tab_rows = _pick(batch * ctx_len, 256)
    cos_id = jnp.ones((tab_rows, LANES), F32)
    sin_id = jnp.zeros((tab_rows, LANES), F32)
    lane = jnp.arange(LANES)
    group_mean = jnp.where((lane[:, None] // DA_HALF) == (lane[None, :] // DA_HALF),
                           1.0 / DA_HALF, 0.0).astype(BF16)

    xl = x.reshape(batch * seq, d)
    xc = ctx.reshape(batch * ctx_len, d)
    huge = batch * max(seq, ctx_len) * 2

    for l in range(depth):
        need_ctx = l < depth - 1
        lam_init = 0.8 - 0.6 * math.exp(-0.3 * l)
        mod = mod_all[l].reshape(n_mod_rows, 1, 6 * d)
        g1 = norm1_g[l].reshape(1, d)
        g2 = norm2_g[l].reshape(1, d)
        w_in_l = _layout_w_in(w_in[l]).astype(BF16)
        w_out_l = w_out[l].astype(BF16)
        w_gu_l = ffn_w_gu[l].astype(BF16)
        w_down_l = ffn_w_down[l].astype(BF16)
        wq_l = jnp.pad(mla_w_uq[l].reshape(-1, MLA_HEADS, MLA_QK),
                       ((0, 0), (0, 0), (0, MLA_PAD - MLA_QK))).reshape(-1, MLA_HEADS * MLA_PAD).astype(BF16)
        wkv_l = mla_w_ukv[l].astype(BF16)
        gq_da = jnp.tile(da_qk_g[l, 0], 2).reshape(1, LANES)
        gk_da = jnp.tile(da_qk_g[l, 1], 2).reshape(1, LANES)
        gq_mla = _pad_cols(mla_qk_g[l, 0].reshape(1, MLA_QK), MLA_PAD)
        gk_mla = _pad_cols(mla_qk_g[l, 1].reshape(1, MLA_QK), MLA_PAD)
        gcq = mla_q_norm_g[l].reshape(1, -1)
        gckv = mla_kv_norm_g[l].reshape(1, -1)
        gate_b = _pad_cols(ml_gate_b[l].reshape(1, -1), LANES)
        conv_w = ml_conv_w[l]
        conv_b = ml_conv_b[l].reshape(1, -1)
        ml_gain = ml_out_g[l].reshape(1, -1)
        da_gain = da_out_g[l].reshape(1, -1)
        lam_p = da_lambda[l]

        p_l, g_l = _in_proj(xl, mod, g1, w_in_l, seq, 0)
        p_c, g_c = _in_proj(xc, mod, g1, w_in_l, huge, batch)

        qd_l, kd_l = _da_prep(p_l, cos_da, sin_da, gq_da, gk_da, group_mean)
        qd_c, kd_c = _da_prep(p_c, cos_id, sin_id, gq_da, gk_da, group_mean)
        da_l = _attention("da", qd_l, [(kd_l, p_l, C_DAV), (kd_c, p_c, C_DAV)], batch,
                          [lam_p, da_gain], lam_init)
        qm_l, km_l, vm_l = _mla_prep(p_l, cos_mla, sin_mla, gcq, gckv, gq_mla, gk_mla, wq_l, wkv_l)
        qm_c, km_c, vm_c = _mla_prep(p_c, cos_id, sin_id, gcq, gckv, gq_mla, gk_mla, wq_l, wkv_l)
        mla_l = _attention("mla", qm_l, [(km_l, vm_l, 0), (km_c, vm_c, 0)], batch, [])

        p3_l = p_l.reshape(batch, seq, P_COLS)
        p3_c = p_c.reshape(batch, ctx_len, P_COLS)
        qk_l = _ml_conv(p3_l, conv_w, conv_b)
        qk_c = _ml_conv(p3_c, conv_w, conv_b)
        gt_l, gg_l = _ml_gates(g_l.reshape(batch, seq, LANES), gate_b)
        gt_c, gg_c = _ml_gates(g_c.reshape(batch, ctx_len, LANES), gate_b)
        s0 = (jnp.zeros((batch * ML_HEADS, ML_DK, 2 * ML_DV), F32),
              jnp.zeros((batch * ML_HEADS, 8, LANES), F32))
        h_cf, c_f, m_f = _ml_scan(qk_c, p3_c, gt_c, gg_c, s0, False)
        h_lf, _, _ = _ml_scan(qk_l, p3_l, gt_l, gg_l, (c_f, m_f), False)
        if need_ctx:
            ml_c, c_b, m_b = _ml_scan(qk_c, p3_c, gt_c, gg_c, s0, True, (h_cf, ml_gain))
        else:
            _, c_b, m_b = _ml_scan(qk_c, p3_c, gt_c, gg_c, s0, True)
        ml_l, _, _ = _ml_scan(qk_l, p3_l, gt_l, gg_l, (c_b, m_b), True, (h_lf, ml_gain))

        h_l = _out_proj(da_l, ml_l.reshape(batch * seq, -1), mla_l, w_out_l, xl, mod, seq, 0)
        xl = _ffn(h_l, mod, g2, w_gu_l, w_down_l, seq, 0)
        if need_ctx:
            da_c = _attention("da", qd_c, [(kd_c, p_c, C_DAV)], batch, [lam_p, da_gain], lam_init)
            mla_c = _attention("mla", qm_c, [(km_c, vm_c, 0)], batch, [])
            h_c = _out_proj(da_c, ml_c.reshape(batch * ctx_len, -1), mla_c, w_out_l, xc, mod, huge, batch)
            xc = _ffn(h_c, mod, g2, w_gu_l, w_down_l, huge, batch)

    return xl.reshape(batch, seq, d).astype(in_dtype)
```

```python
import functools
import math

import jax
import jax.numpy as jnp
from jax import lax
from jax.experimental import pallas as pl
from jax.experimental.pallas import tpu as pltpu

F32 = jnp.float32
BF16 = jnp.bfloat16

DA_HEADS = 4
DA_HALF = 64
DA_V = 2 * DA_HALF
ML_HEADS = 4
ML_DK = 128
ML_DV = 128
MLA_HEADS = 8
MLA_NOPE = 128
MLA_ROPE = 64
MLA_V = 128
MLA_QK = MLA_NOPE + MLA_ROPE
MLA_PAD = 256
GRID_W = 64
ROPE_DIM = 64
ROPE_BASE = 10000.0
EPS = 1e-6
LOG2E = 1.4426950408889634

LANES = 128
ML_CHUNK = 128
VMEM_LIMIT = 56 * 1024 * 1024

C_DAQ, C_DAK, C_DAV = 0, 512, 1024
C_MLQ, C_MLK, C_MLV, C_MLO = 1536, 2048, 2560, 3072
C_CQ, C_CKV, C_KPE, C_GATE = 3584, 4096, 4352, 4480
P_COLS = 4608


def _cparams(*sem):
    return pltpu.CompilerParams(dimension_semantics=sem, vmem_limit_bytes=VMEM_LIMIT)


def _pick(n, pref):
    if n <= pref:
        return n
    t = pref
    while n % t:
        t //= 2
    return t


def _mod_kernel(c_ref, w_ref, b_ref, o_ref):
    c = c_ref[...]
    s = (c / (1.0 + jnp.exp(-c))).astype(BF16)
    o_ref[...] = jnp.dot(s, w_ref[...].astype(BF16), preferred_element_type=F32) + b_ref[...]


def _modulation(cc, mod_w, mod_b):
    depth, d, n = mod_w.shape
    tn = _pick(n, 1024)
    return pl.pallas_call(
        _mod_kernel,
        out_shape=jax.ShapeDtypeStruct((depth, cc.shape[0], n), F32),
        grid=(depth, n // tn),
        in_specs=[
            pl.BlockSpec((cc.shape[0], d), lambda l, j: (0, 0)),
            pl.BlockSpec((None, d, tn), lambda l, j: (l, 0, j)),
            pl.BlockSpec((None, 1, tn), lambda l, j: (l, 0, j)),
        ],
        out_specs=pl.BlockSpec((None, cc.shape[0], tn), lambda l, j: (l, 0, j)),
        compiler_params=_cparams("arbitrary", "arbitrary"),
        name="modulation",
    )(cc, mod_w, mod_b.reshape(depth, 1, n))


def _norm_rows(x_ref, g_ref, shift, scale1, out_ref, rc):
    tm = x_ref.shape[0]
    g = g_ref[...]

    def body(r, carry):
        rows = pl.ds(pl.multiple_of(r * rc, rc), rc)
        x = x_ref[rows, :]
        ms = jnp.mean(x * x, axis=-1, keepdims=True)
        y = x * lax.rsqrt(ms + EPS) * g
        out_ref[rows, :] = (y * scale1 + shift).astype(out_ref.dtype)
        return carry

    lax.fori_loop(0, tm // rc, body, 0)


def _in_proj_kernel(x_ref, mod_ref, g_ref, w_ref, p_ref, gate_ref, xn_ref, *, d, rc):
    n = pl.program_id(1)

    @pl.when(n == 0)
    def _():
        shift = mod_ref[:, 0:d]
        scale1 = 1.0 + mod_ref[:, d:2 * d]
        _norm_rows(x_ref, g_ref, shift, scale1, xn_ref, rc)

    acc = jnp.dot(xn_ref[...], w_ref[...], preferred_element_type=F32)
    p_ref[...] = acc.astype(p_ref.dtype)

    @pl.when(n == pl.num_programs(1) - 1)
    def _():
        gate_ref[...] = acc[:, acc.shape[1] - LANES:]


def _in_proj(x, mod, g, w, rows_per_mod, mod_base):
    r, d = x.shape
    n = w.shape[1]
    tm = _pick(min(r, rows_per_mod), 1024)
    tn = 768
    assert n % tn == 0
    return pl.pallas_call(
        functools.partial(_in_proj_kernel, d=d, rc=_pick(tm, 64)),
        out_shape=(jax.ShapeDtypeStruct((r, n), BF16), jax.ShapeDtypeStruct((r, LANES), F32)),
        grid=(r // tm, n // tn),
        in_specs=[
            pl.BlockSpec((tm, d), lambda i, j: (i, 0)),
            pl.BlockSpec((None, 1, 6 * d), lambda i, j: (i * tm // rows_per_mod + mod_base, 0, 0)),
            pl.BlockSpec((1, d), lambda i, j: (0, 0)),
            pl.BlockSpec((d, tn), lambda i, j: (0, j)),
        ],
        out_specs=(
            pl.BlockSpec((tm, tn), lambda i, j: (i, j)),
            pl.BlockSpec((tm, LANES), lambda i, j: (i, 0)),
        ),
        scratch_shapes=[pltpu.VMEM((tm, d), BF16)],
        compiler_params=_cparams("arbitrary", "arbitrary"),
        name="in_proj",
    )(x, mod, g, w)


def _swap16(x):
    lane = lax.broadcasted_iota(jnp.int32, x.shape, 1)
    fwd = pltpu.roll(x, LANES - 16, 1)
    bwd = pltpu.roll(x, 16, 1)
    return jnp.where((lane % 32) < 16, fwd, bwd)


def _rope(x, cos, sin):
    return x * cos + _swap16(x) * sin


def _group_mean_sq(x, gm):
    x2 = x * x
    hi = x2.astype(BF16)
    r1 = x2 - hi.astype(F32)
    mid = r1.astype(BF16)
    lo = (r1 - mid.astype(F32)).astype(BF16)
    return (jnp.dot(hi, gm, preferred_element_type=F32)
            + jnp.dot(mid, gm, preferred_element_type=F32)
            + jnp.dot(lo, gm, preferred_element_type=F32))


def _da_prep_kernel(q_ref, k_ref, cos_ref, sin_ref, gq_ref, gk_ref, gm_ref, qo_ref, ko_ref, *, qscale):
    cos = cos_ref[...]
    sin = sin_ref[...]
    gm = gm_ref[...]
    lane = lax.broadcasted_iota(jnp.int32, cos.shape, 1)
    first = lane < DA_HALF
    for h in range(DA_HEADS):
        sl = slice(h * LANES, (h + 1) * LANES)
        q = q_ref[:, sl].astype(F32)
        q = q * lax.rsqrt(_group_mean_sq(q, gm) + EPS) * gq_ref[...]
        q = _rope(q, cos, sin) * qscale
        qo_ref[:, 2 * h * LANES:(2 * h + 1) * LANES] = jnp.where(first, q, 0.0).astype(qo_ref.dtype)
        qo_ref[:, (2 * h + 1) * LANES:(2 * h + 2) * LANES] = jnp.where(first, 0.0, q).astype(qo_ref.dtype)
        k = k_ref[:, sl].astype(F32)
        k = k * lax.rsqrt(_group_mean_sq(k, gm) + EPS) * gk_ref[...]
        ko_ref[:, sl] = _rope(k, cos, sin).astype(ko_ref.dtype)


def _da_prep(p, cos, sin, gq, gk, gm):
    r = p.shape[0]
    tm = _pick(r, 256)
    tab_blocks = cos.shape[0] // tm
    w = DA_HEADS * LANES
    qscale = DA_HALF ** -0.5 * LOG2E
    return pl.pallas_call(
        functools.partial(_da_prep_kernel, qscale=qscale),
        out_shape=(jax.ShapeDtypeStruct((r, 2 * w), BF16), jax.ShapeDtypeStruct((r, w), BF16)),
        grid=(r // tm,),
        in_specs=[
            pl.BlockSpec((tm, w), lambda i: (i, C_DAQ // w)),
            pl.BlockSpec((tm, w), lambda i: (i, C_DAK // w)),
            pl.BlockSpec((tm, LANES), lambda i: (i % tab_blocks, 0)),
            pl.BlockSpec((tm, LANES), lambda i: (i % tab_blocks, 0)),
            pl.BlockSpec((1, LANES), lambda i: (0, 0)),
            pl.BlockSpec((1, LANES), lambda i: (0, 0)),
            pl.BlockSpec((LANES, LANES), lambda i: (0, 0)),
        ],
        out_specs=(
            pl.BlockSpec((tm, 2 * w), lambda i: (i, 0)),
            pl.BlockSpec((tm, w), lambda i: (i, 0)),
        ),
        compiler_params=_cparams("arbitrary"),
        name="da_prep",
    )(p, p, cos, sin, gq, gk, gm)


def _mla_prep_kernel(cq_ref, ckv_ref, kpe_ref, cos_ref, sin_ref, gcq_ref, gckv_ref, gq_ref, gk_ref,
                     wq_ref, wkv_ref, qo_ref, ko_ref, vo_ref, *, qscale):
    cos = cos_ref[...]
    sin = sin_ref[...]
    lane = lax.broadcasted_iota(jnp.int32, cos.shape, 1)
    inv_qk = 1.0 / MLA_QK

    cq = cq_ref[...].astype(F32)
    cq = cq * lax.rsqrt(jnp.mean(cq * cq, axis=-1, keepdims=True) + EPS) * gcq_ref[...]
    qf = jnp.dot(cq.astype(BF16), wq_ref[...], preferred_element_type=F32)

    ckv = ckv_ref[...].astype(F32)
    ckv = ckv * lax.rsqrt(jnp.mean(ckv * ckv, axis=-1, keepdims=True) + EPS) * gckv_ref[...]
    kvf = jnp.dot(ckv.astype(BF16), wkv_ref[...], preferred_element_type=F32)

    kpe = jnp.where(lane < MLA_ROPE, kpe_ref[...].astype(F32), 0.0)
    kpe_ss = jnp.sum(kpe * kpe, axis=-1, keepdims=True)
    kpe_rot = _rope(kpe * gk_ref[:, LANES:], cos, sin)

    for h in range(MLA_HEADS):
        q1 = qf[:, h * MLA_PAD:h * MLA_PAD + LANES]
        q2 = qf[:, h * MLA_PAD + LANES:(h + 1) * MLA_PAD]
        ms = (jnp.sum(q1 * q1, axis=-1, keepdims=True) + jnp.sum(q2 * q2, axis=-1, keepdims=True)) * inv_qk
        rq = lax.rsqrt(ms + EPS) * qscale
        qo_ref[:, h * MLA_PAD:h * MLA_PAD + LANES] = (q1 * rq * gq_ref[:, :LANES]).astype(qo_ref.dtype)
        q2 = _rope(q2 * rq * gq_ref[:, LANES:], cos, sin)
        qo_ref[:, h * MLA_PAD + LANES:(h + 1) * MLA_PAD] = q2.astype(qo_ref.dtype)

        kn = kvf[:, h * MLA_PAD:h * MLA_PAD + LANES]
        ms = (jnp.sum(kn * kn, axis=-1, keepdims=True) + kpe_ss) * inv_qk
        rk = lax.rsqrt(ms + EPS)
        ko_ref[:, h * MLA_PAD:h * MLA_PAD + LANES] = (kn * rk * gk_ref[:, :LANES]).astype(ko_ref.dtype)
        ko_ref[:, h * MLA_PAD + LANES:(h + 1) * MLA_PAD] = (kpe_rot * rk).astype(ko_ref.dtype)
        vo_ref[:, h * MLA_V:(h + 1) * MLA_V] = kvf[:, h * MLA_PAD + LANES:(h + 1) * MLA_PAD].astype(vo_ref.dtype)


def _mla_prep(p, cos, sin, gcq, gckv, gq, gk, wq, wkv):
    r = p.shape[0]
    tm = _pick(r, 256)
    tab_blocks = cos.shape[0] // tm
    q_rank = wq.shape[0]
    kv_rank = wkv.shape[0]
    wide = MLA_HEADS * MLA_PAD
    qscale = MLA_QK ** -0.5 * LOG2E
    const = lambda i: (0, 0)
    return pl.pallas_call(
        functools.partial(_mla_prep_kernel, qscale=qscale),
        out_shape=(jax.ShapeDtypeStruct((r, wide), BF16), jax.ShapeDtypeStruct((r, wide), BF16),
                   jax.ShapeDtypeStruct((r, MLA_HEADS * MLA_V), BF16)),
        grid=(r // tm,),
        in_specs=[
            pl.BlockSpec((tm, q_rank), lambda i: (i, C_CQ // q_rank)),
            pl.BlockSpec((tm, kv_rank), lambda i: (i, C_CKV // kv_rank)),
            pl.BlockSpec((tm, LANES), lambda i: (i, C_KPE // LANES)),
            pl.BlockSpec((tm, LANES), lambda i: (i % tab_blocks, 0)),
            pl.BlockSpec((tm, LANES), lambda i: (i % tab_blocks, 0)),
            pl.BlockSpec((1, q_rank), const),
            pl.BlockSpec((1, kv_rank), const),
            pl.BlockSpec((1, MLA_PAD), const),
            pl.BlockSpec((1, MLA_PAD), const),
            pl.BlockSpec((q_rank, wide), const),
            pl.BlockSpec((kv_rank, wide), const),
        ],
        out_specs=(
            pl.BlockSpec((tm, wide), lambda i: (i, 0)),
            pl.BlockSpec((tm, wide), lambda i: (i, 0)),
            pl.BlockSpec((tm, MLA_HEADS * MLA_V), lambda i: (i, 0)),
        ),
        compiler_params=_cparams("arbitrary"),
        name="mla_prep",
    )(p, p, p, cos, sin, gcq, gckv, gq, gk, wq, wkv)


def _attn_core(q_ref, k_refs, v_refs, n_comp, dq, kc):
    chunks = []
    for ri, kr in enumerate(k_refs):
        rows = kr.shape[0]
        step = min(kc, rows)
        for r0 in range(0, rows, step):
            chunks.append((ri, r0, step))
    outs = []
    for c in range(n_comp):
        q = q_ref[:, c * dq:(c + 1) * dq]
        m = l = acc = None
        for ri, r0, step in chunks:
            s = lax.dot_general(q, k_refs[ri][r0:r0 + step, :], (((1,), (1,)), ((), ())),
                                preferred_element_type=F32)
            cm = jnp.max(s, axis=-1, keepdims=True)
            m_new = cm if m is None else jnp.maximum(m, cm)
            p = jnp.exp2(s - m_new)
            ps = jnp.sum(p, axis=-1, keepdims=True)
            pv = jnp.dot(p.astype(BF16), v_refs[ri][r0:r0 + step, :], preferred_element_type=F32)
            if m is None:
                l, acc = ps, pv
            else:
                alpha = jnp.exp2(m - m_new)
                l = alpha * l + ps
                acc = alpha * acc + pv
            m = m_new
        outs.append(acc / l)
    return outs


def _da_attn_kernel(*refs, n_kv, lam_init, kc):
    q_ref = refs[0]
    k_refs = refs[1:1 + n_kv]
    v_refs = refs[1 + n_kv:1 + 2 * n_kv]
    lam_ref, g_ref, o_ref = refs[1 + 2 * n_kv:]
    lp = lam_ref[...]
    lam = (jnp.exp(jnp.sum(lp[0:1] * lp[1:2], axis=-1, keepdims=True))
           - jnp.exp(jnp.sum(lp[2:3] * lp[3:4], axis=-1, keepdims=True)) + lam_init)
    o0, o1 = _attn_core(q_ref, k_refs, v_refs, 2, LANES, kc)
    o = o0 - lam * o1
    o = o * lax.rsqrt(jnp.mean(o * o, axis=-1, keepdims=True) + EPS) * g_ref[...]
    o_ref[...] = (o * (1.0 - lam_init)).astype(o_ref.dtype)


def _mla_attn_kernel(*refs, n_kv, kc):
    q_ref = refs[0]
    k_refs = refs[1:1 + n_kv]
    v_refs = refs[1 + n_kv:1 + 2 * n_kv]
    (o_ref,) = refs[1 + 2 * n_kv:]
    (o,) = _attn_core(q_ref, k_refs, v_refs, 1, MLA_PAD, kc)
    o_ref[...] = o.astype(o_ref.dtype)


def _attention(kind, q, kvs, batch, extra, lam_init=0.0):
    heads = DA_HEADS if kind == "da" else MLA_HEADS
    qw = 2 * LANES if kind == "da" else MLA_PAD
    dq = LANES if kind == "da" else MLA_PAD
    dv = DA_V if kind == "da" else MLA_V
    rq = q.shape[0]
    tq = _pick(rq // batch, 1024)
    nq = rq // batch // tq
    kc = 1024
    k_specs, v_specs, k_args, v_args = [], [], [], []
    for k, v, vcol in kvs:
        kl = k.shape[0] // batch
        k_specs.append(pl.BlockSpec((kl, dq), lambda b, h, i: (b, h)))
        v_specs.append(pl.BlockSpec((kl, dv), lambda b, h, i, vcol=vcol: (b, vcol // dv + h)))
        k_args.append(k)
        v_args.append(v)
    extra_specs = [pl.BlockSpec(e.shape, lambda b, h, i: (0, 0)) for e in extra]
    if kind == "da":
        body = functools.partial(_da_attn_kernel, n_kv=len(kvs), lam_init=lam_init, kc=kc)
    else:
        body = functools.partial(_mla_attn_kernel, n_kv=len(kvs), kc=kc)
    return pl.pallas_call(
        body,
        out_shape=jax.ShapeDtypeStruct((rq, heads * dv), BF16),
        grid=(batch, heads, nq),
        in_specs=[pl.BlockSpec((tq, qw), lambda b, h, i: (b * nq + i, h))] + k_specs + v_specs + extra_specs,
        out_specs=pl.BlockSpec((tq, dv), lambda b, h, i: (b * nq + i, h)),
        compiler_params=_cparams("arbitrary", "arbitrary", "arbitrary"),
        name=kind + "_attn",
    )(q, *k_args, *v_args, *extra)


def _ml_conv_kernel(x_ref, w_ref, b_ref, o_ref, *, kscale, n_q_blocks):
    j = pl.program_id(1)
    x = x_ref[...].astype(F32)
    s = x.shape[0]
    t = lax.broadcasted_iota(jnp.int32, x.shape, 0)
    prev = jnp.where(t == 0, 0.0, pltpu.roll(x, 1, 0))
    nxt = jnp.where(t == s - 1, 0.0, pltpu.roll(x, s - 1, 0))
    y = b_ref[...] + prev * w_ref[0:1, :] + x * w_ref[1:2, :] + nxt * w_ref[2:3, :]
    y = y / (1.0 + jnp.exp(-y))
    scale = jnp.where(j >= n_q_blocks, kscale, 1.0)
    o_ref[...] = (y * scale).astype(o_ref.dtype)


def _ml_conv(p3, conv_w, conv_b):
    nseq, s, _ = p3.shape
    nblk = 2 * ML_HEADS * ML_DK // LANES
    return pl.pallas_call(
        functools.partial(_ml_conv_kernel, kscale=ML_DK ** -0.5, n_q_blocks=nblk // 2),
        out_shape=jax.ShapeDtypeStruct((nseq, s, nblk * LANES), BF16),
        grid=(nseq, nblk),
        in_specs=[
            pl.BlockSpec((None, s, LANES), lambda b, j: (b, 0, C_MLQ // LANES + j)),
            pl.BlockSpec((3, LANES), lambda b, j: (0, j)),
            pl.BlockSpec((1, LANES), lambda b, j: (0, j)),
        ],
        out_specs=pl.BlockSpec((None, s, LANES), lambda b, j: (b, 0, j)),
        compiler_params=_cparams("arbitrary", "arbitrary"),
        name="ml_conv",
    )(p3, conv_w, conv_b)


def _ml_gate_kernel(g_ref, b_ref, gt_ref, gg_ref, *, rc):
    s = g_ref.shape[0]
    for r0 in range(0, s, rc):
        g = g_ref[r0:r0 + rc, :] + b_ref[...]
        lane = lax.broadcasted_iota(jnp.int32, g.shape, 1)
        is_forget = ((lane % (2 * ML_HEADS)) >= ML_HEADS) & (lane < 4 * ML_HEADS)
        lf = jnp.minimum(g, 0.0) - jnp.log(1.0 + jnp.exp(-jnp.abs(g)))
        out = jnp.where(is_forget, lf, g)
        gt_ref[r0:r0 + rc, :] = out
        gg_ref[:, r0:r0 + rc] = out.T[0:4 * ML_HEADS, :]


def _ml_gates(g3, gate_b):
    nseq, s, _ = g3.shape
    return pl.pallas_call(
        functools.partial(_ml_gate_kernel, rc=_pick(s, 512)),
        out_shape=(jax.ShapeDtypeStruct((nseq, s, LANES), F32),
                   jax.ShapeDtypeStruct((nseq, 4 * ML_HEADS, s), F32)),
        grid=(nseq,),
        in_specs=[
            pl.BlockSpec((None, s, LANES), lambda b: (b, 0, 0)),
            pl.BlockSpec((1, LANES), lambda b: (0, 0)),
        ],
        out_specs=(
            pl.BlockSpec((None, s, LANES), lambda b: (b, 0, 0)),
            pl.BlockSpec((None, 4 * ML_HEADS, s), lambda b: (b, 0, 0)),
        ),
        compiler_params=_cparams("arbitrary"),
        name="ml_gates",
    )(g3, gate_b)


def _split3(x):
    hi = x.astype(BF16)
    r1 = x - hi.astype(F32)
    mid = r1.astype(BF16)
    lo = (r1 - mid.astype(F32)).astype(BF16)
    return hi, mid, lo


def _ml_scan_kernel(*refs, nb, reverse, post):
    if post:
        (q_ref, k_ref, v_ref, gt_ref, gg_ref, c0_ref, m0_ref, hp_ref, og_ref, gain_ref,
         h_ref, c_out_ref, m_out_ref, c_sc, m_sc) = refs
    else:
        (q_ref, k_ref, v_ref, gt_ref, gg_ref, c0_ref, m0_ref,
         h_ref, c_out_ref, m_out_ref, c_sc, m_sc) = refs
    step = pl.program_id(0)
    L = q_ref.shape[1]
    nh = ML_HEADS

    @pl.when(step == 0)
    def _():
        c_sc[...] = c0_ref[...]
        m_sc[...] = m0_ref[...]

    row = lax.broadcasted_iota(jnp.int32, (L, L), 0)
    col = lax.broadcasted_iota(jnp.int32, (L, L), 1)
    incl = (col >= row) if reverse else (col <= row)
    incl_b = jnp.where(incl, 1.0, 0.0).astype(BF16)
    incl_t = jnp.where((row >= col) if reverse else (row <= col), 1.0, 0.0).astype(BF16)
    last = 0 if reverse else L - 1
    ones = jnp.ones((L, ML_DV), BF16)

    for b in range(nb):
        gt = gt_ref[b]
        gg = gg_ref[b]
        bc_col = sum(jnp.dot(incl_b, part, preferred_element_type=F32) for part in _split3(gt))
        bc_row = sum(jnp.dot(part, incl_t, preferred_element_type=F32) for part in _split3(gg))
        for h in range(nh):
            bh = b * nh + h
            ig_lane = (2 * nh if reverse else 0) + h
            lf_lane = ig_lane + nh
            hs = slice(h * ML_DK, (h + 1) * ML_DK)
            q = q_ref[b, :, hs]
            k = k_ref[b, :, hs]
            v = v_ref[b, :, hs]
            c_aug = c_sc[bh]
            m_prev = m_sc[bh][0:1, 0:1]

            bcum_col = bc_col[:, lf_lane:lf_lane + 1]
            bcum_row = bc_row[lf_lane:lf_lane + 1, :]
            ig_col = gt[:, ig_lane:ig_lane + 1]
            ig_row = gg[ig_lane:ig_lane + 1, :]
            total = bcum_col[last:last + 1, :]

            log_d = jnp.where(incl, bcum_col - bcum_row + ig_row, -jnp.inf)
            log_prev = bcum_col + m_prev
            m_t = jnp.maximum(log_prev, jnp.max(log_d, axis=-1, keepdims=True))
            dmat = jnp.exp(log_d - m_t)
            w_prev = jnp.exp(log_prev - m_t)
            s = lax.dot_general(q, k, (((1,), (1,)), ((), ())), preferred_element_type=F32) * dmat
            v_aug = jnp.concatenate([v, ones], axis=-1)
            tot = (jnp.dot(s.astype(BF16), v_aug, preferred_element_type=F32)
                   + w_prev * jnp.dot(q, c_aug.astype(BF16), preferred_element_type=F32))
            num = tot[:, :ML_DV]
            den = tot[:, ML_DV:]
            hv = num / jnp.maximum(jnp.abs(den), jnp.exp(-m_t))

            m_new = m_t[last:last + 1, :]
            w_s = jnp.exp(total - bcum_col + ig_col - m_new)
            decay = jnp.exp(total + m_prev - m_new)
            kv = lax.dot_general(k, (w_s * v_aug.astype(F32)).astype(BF16), (((0,), (0,)), ((), ())),
                                 preferred_element_type=F32)
            c_sc[bh] = decay * c_aug + kv
            m_sc[bh] = jnp.broadcast_to(m_new, m_sc.shape[1:])

            if post:
                hv = hv + hp_ref[b, :, hs]
                hv = hv * lax.rsqrt(jnp.mean(hv * hv, axis=-1, keepdims=True) + EPS) * gain_ref[:, hs]
                og = og_ref[b, :, hs].astype(F32)
                hv = hv / (1.0 + jnp.exp(-og))
            h_ref[b, :, hs] = hv.astype(h_ref.dtype)

    @pl.when(step == pl.num_programs(0) - 1)
    def _():
        c_out_ref[...] = c_sc[...]
        m_out_ref[...] = m_sc[...]


def _ml_scan(qk3, p3, gt3, gg3, state, reverse, post_args=None):
    nb, s, _ = qk3.shape
    L = min(ML_CHUNK, s)
    nc = s // L
    w = ML_HEADS * ML_DK
    c0, m0 = state
    post = post_args is not None
    cidx = (lambda c: nc - 1 - c) if reverse else (lambda c: c)
    in_specs = [
        pl.BlockSpec((nb, L, w), lambda c: (0, cidx(c), 0)),
        pl.BlockSpec((nb, L, w), lambda c: (0, cidx(c), 1)),
        pl.BlockSpec((nb, L, w), lambda c: (0, cidx(c), C_MLV // w)),
        pl.BlockSpec((nb, L, LANES), lambda c: (0, cidx(c), 0)),
        pl.BlockSpec((nb, 4 * ML_HEADS, L), lambda c: (0, 0, cidx(c))),
        pl.BlockSpec(c0.shape, lambda c: (0, 0, 0)),
        pl.BlockSpec(m0.shape, lambda c: (0, 0, 0)),
    ]
    args = [qk3, qk3, p3, gt3, gg3, c0, m0]
    if post:
        h_prev, gain = post_args
        in_specs += [
            pl.BlockSpec((nb, L, w), lambda c: (0, cidx(c), 0)),
            pl.BlockSpec((nb, L, w), lambda c: (0, cidx(c), C_MLO // w)),
            pl.BlockSpec((1, w), lambda c: (0, 0)),
        ]
        args += [h_prev, p3, gain]
    return pl.pallas_call(
        functools.partial(_ml_scan_kernel, nb=nb, reverse=reverse, post=post),
        out_shape=(jax.ShapeDtypeStruct((nb, s, w), BF16 if post else F32),
                   jax.ShapeDtypeStruct(c0.shape, F32), jax.ShapeDtypeStruct(m0.shape, F32)),
        grid=(nc,),
        in_specs=in_specs,
        out_specs=(
            pl.BlockSpec((nb, L, w), lambda c: (0, cidx(c), 0)),
            pl.BlockSpec(c0.shape, lambda c: (0, 0, 0)),
            pl.BlockSpec(m0.shape, lambda c: (0, 0, 0)),
        ),
        scratch_shapes=[pltpu.VMEM(c0.shape, F32), pltpu.VMEM(m0.shape, F32)],
        compiler_params=_cparams("arbitrary"),
        name="ml_scan",
    )(*args)


def _out_proj_kernel(da_ref, ml_ref, mla_ref, w_ref, x_ref, gate_ref, h_ref):
    n_da = da_ref.shape[1]
    n_ml = ml_ref.shape[1]
    acc = jnp.dot(da_ref[...], w_ref[0:n_da, :], preferred_element_type=F32)
    acc += jnp.dot(ml_ref[...], w_ref[n_da:n_da + n_ml, :], preferred_element_type=F32)
    acc += jnp.dot(mla_ref[...], w_ref[n_da + n_ml:, :], preferred_element_type=F32)
    h_ref[...] = x_ref[...] + gate_ref[...] * acc


def _out_proj(da, ml, mla, w, x, mod, rows_per_mod, mod_base):
    r, d = x.shape
    tm = _pick(min(r, rows_per_mod), 1024)
    tn = _pick(d, 1024)
    nt = d // tn
    return pl.pallas_call(
        _out_proj_kernel,
        out_shape=jax.ShapeDtypeStruct((r, d), F32),
        grid=(nt, r // tm),
        in_specs=[
            pl.BlockSpec((tm, da.shape[1]), lambda j, i: (i, 0)),
            pl.BlockSpec((tm, ml.shape[1]), lambda j, i: (i, 0)),
            pl.BlockSpec((tm, mla.shape[1]), lambda j, i: (i, 0)),
            pl.BlockSpec((w.shape[0], tn), lambda j, i: (0, j)),
            pl.BlockSpec((tm, tn), lambda j, i: (i, j)),
            pl.BlockSpec((None, 1, tn), lambda j, i: (i * tm // rows_per_mod + mod_base, 0, 2 * nt + j)),
        ],
        out_specs=pl.BlockSpec((tm, tn), lambda j, i: (i, j)),
        compiler_params=_cparams("arbitrary", "arbitrary"),
        name="out_proj",
    )(da, ml, mla, w, x, mod)


def _ffn_kernel(h_ref, mod_ref, g_ref, wg_ref, wu_ref, wd_ref, o_ref, hn_ref, *, d, rc):
    j = pl.program_id(1)

    @pl.when(j == 0)
    def _():
        shift = mod_ref[:, 3 * d:4 * d]
        scale1 = 1.0 + mod_ref[:, 4 * d:5 * d]
        _norm_rows(h_ref, g_ref, shift, scale1, hn_ref, rc)
        o_ref[...] = h_ref[...]

    hn = hn_ref[...]
    gate = jnp.dot(hn, wg_ref[...], preferred_element_type=F32)
    up = jnp.dot(hn, wu_ref[...], preferred_element_type=F32)
    act = (gate / (1.0 + jnp.exp(-gate)) * up).astype(BF16)
    o_ref[...] += mod_ref[:, 5 * d:6 * d] * jnp.dot(act, wd_ref[...], preferred_element_type=F32)


def _ffn(h, mod, g, w_gu, w_down, rows_per_mod, mod_base):
    r, d = h.shape
    ff = w_down.shape[0]
    tm = _pick(min(r, rows_per_mod), 512)
    th = 512 if ff % 512 == 0 else 256
    nh = ff // th
    return pl.pallas_call(
        functools.partial(_ffn_kernel, d=d, rc=_pick(tm, 64)),
        out_shape=jax.ShapeDtypeStruct((r, d), F32),
        grid=(r // tm, nh),
        in_specs=[
            pl.BlockSpec((tm, d), lambda i, j: (i, 0)),
            pl.BlockSpec((None, 1, 6 * d), lambda i, j: (i * tm // rows_per_mod + mod_base, 0, 0)),
            pl.BlockSpec((1, d), lambda i, j: (0, 0)),
            pl.BlockSpec((d, th), lambda i, j: (0, j)),
            pl.BlockSpec((d, th), lambda i, j: (0, nh + j)),
            pl.BlockSpec((th, d), lambda i, j: (j, 0)),
        ],
        out_specs=pl.BlockSpec((tm, d), lambda i, j: (i, 0)),
        scratch_shapes=[pltpu.VMEM((tm, d), BF16)],
        compiler_params=_cparams("arbitrary", "arbitrary"),
        name="ffn",
    )(h, mod, g, w_gu, w_gu, w_down)


def _rope_tables(n_lat):
    pos = jnp.arange(n_lat, dtype=jnp.int32)
    r = (pos // GRID_W).astype(F32)
    c = (pos % GRID_W).astype(F32)
    half = ROPE_DIM // 4
    inv = ROPE_BASE ** (-jnp.arange(half, dtype=F32) / half)
    ar = r[:, None] * inv
    ac = c[:, None] * inv
    cos64 = jnp.concatenate([jnp.cos(ar), jnp.cos(ar), jnp.cos(ac), jnp.cos(ac)], axis=-1)
    sin64 = jnp.concatenate([-jnp.sin(ar), jnp.sin(ar), -jnp.sin(ac), jnp.sin(ac)], axis=-1)
    return cos64, sin64


def _pad_cols(a, width):
    return jnp.pad(a, ((0, 0), (0, width - a.shape[1])))


def _layout_w_in(w):
    gates0 = C_CQ
    gates1 = gates0 + 4 * ML_HEADS
    main = w[:, :gates0]
    gates = w[:, gates0:gates1]
    rest = w[:, gates1:]
    n_cq_ckv = C_KPE - C_CQ
    cq_ckv = rest[:, :n_cq_ckv]
    kpe = rest[:, n_cq_ckv:]
    return jnp.concatenate([main, cq_ckv, _pad_cols(kpe, LANES), _pad_cols(gates, LANES)], axis=1)


def kernel(x, c, ctx, c_ctx, mod_w, mod_b, norm1_g, norm2_g, w_in, da_qk_g, da_lambda, da_out_g,
           ml_conv_w, ml_conv_b, ml_gate_b, ml_out_g, mla_q_norm_g, mla_kv_norm_g, mla_w_uq, mla_w_ukv,
           mla_qk_g, w_out, ffn_w_gu, ffn_w_down):
    batch, seq, d = x.shape
    ctx_len = ctx.shape[1]
    depth = mod_w.shape[0]
    in_dtype = x.dtype
    assert w_in.shape[2] == C_CQ + 4 * ML_HEADS + (C_KPE - C_CQ) + MLA_ROPE

    n_mod_rows = -(-(batch + 1) // 8) * 8
    cc = jnp.concatenate([c, c_ctx[None, :], jnp.zeros((n_mod_rows - batch - 1, d), F32)], axis=0)
    mod_all = _modulation(cc, mod_w, mod_b)

    cos64, sin64 = _rope_tables(seq)
    cos_da = jnp.tile(cos64, (1, 2))
    sin_da = jnp.tile(sin64, (1, 2))
    cos_mla = jnp.concatenate([cos64, jnp.ones_like(cos64)], axis=-1)
    sin_mla = jnp.concatenate([sin64, jnp.zeros_like(sin64)], axis=-1)
    tab_rows = _pick(batch * ctx_len, 256)
    cos_id = jnp.ones((tab_rows, LANES), F32)
    sin_id = jnp.zeros((tab_rows, LANES), F32)
    lane = jnp.arange(LANES)
    group_mean = jnp.where((lane[:, None] // DA_HALF) == (lane[None, :] // DA_HALF),
                           1.0 / DA_HALF, 0.0).astype(BF16)

    xl = x.reshape(batch * seq, d)
    xc = ctx.reshape(batch * ctx_len, d)
    huge = batch * max(seq, ctx_len) * 2

    for l in range(depth):
        need_ctx = l < depth - 1
        lam_init = 0.8 - 0.6 * math.exp(-0.3 * l)
        mod = mod_all[l].reshape(n_mod_rows, 1, 6 * d)
        g1 = norm1_g[l].reshape(1, d)
        g2 = norm2_g[l].reshape(1, d)
        w_in_l = _layout_w_in(w_in[l]).astype(BF16)
        w_out_l = w_out[l].astype(BF16)
        w_gu_l = ffn_w_gu[l].astype(BF16)
        w_down_l = ffn_w_down[l].astype(BF16)
        wq_l = jnp.pad(mla_w_uq[l].reshape(-1, MLA_HEADS, MLA_QK),
                       ((0, 0), (0, 0), (0, MLA_PAD - MLA_QK))).reshape(-1, MLA_HEADS * MLA_PAD).astype(BF16)
        wkv_l = mla_w_ukv[l].astype(BF16)
        gq_da = jnp.tile(da_qk_g[l, 0], 2).reshape(1, LANES)
        gk_da = jnp.tile(da_qk_g[l, 1], 2).reshape(1, LANES)
        gq_mla = _pad_cols(mla_qk_g[l, 0].reshape(1, MLA_QK), MLA_PAD)
        gk_mla = _pad_cols(mla_qk_g[l, 1].reshape(1, MLA_QK), MLA_PAD)
        gcq = mla_q_norm_g[l].reshape(1, -1)
        gckv = mla_kv_norm_g[l].reshape(1, -1)
        gate_b = _pad_cols(ml_gate_b[l].reshape(1, -1), LANES)
        conv_w = ml_conv_w[l]
        conv_b = ml_conv_b[l].reshape(1, -1)
        ml_gain = ml_out_g[l].reshape(1, -1)
        da_gain = da_out_g[l].reshape(1, -1)
        lam_p = da_lambda[l]

        p_l, g_l = _in_proj(xl, mod, g1, w_in_l, seq, 0)
        p_c, g_c = _in_proj(xc, mod, g1, w_in_l, huge, batch)

        qd_l, kd_l = _da_prep(p_l, cos_da, sin_da, gq_da, gk_da, group_mean)
        qd_c, kd_c = _da_prep(p_c, cos_id, sin_id, gq_da, gk_da, group_mean)
        da_l = _attention("da", qd_l, [(kd_l, p_l, C_DAV), (kd_c, p_c, C_DAV)], batch,
                          [lam_p, da_gain], lam_init)
        qm_l, km_l, vm_l = _mla_prep(p_l, cos_mla, sin_mla, gcq, gckv, gq_mla, gk_mla, wq_l, wkv_l)
        qm_c, km_c, vm_c = _mla_prep(p_c, cos_id, sin_id, gcq, gckv, gq_mla, gk_mla, wq_l, wkv_l)
        mla_l = _attention("mla", qm_l, [(km_l, vm_l, 0), (km_c, vm_c, 0)], batch, [])

        p3_l = p_l.reshape(batch, seq, P_COLS)
        p3_c = p_c.reshape(batch, ctx_len, P_COLS)
        qk_l = _ml_conv(p3_l, conv_w, conv_b)
        qk_c = _ml_conv(p3_c, conv_w, conv_b)
        gt_l, gg_l = _ml_gates(g_l.reshape(batch, seq, LANES), gate_b)
        gt_c, gg_c = _ml_gates(g_c.reshape(batch, ctx_len, LANES), gate_b)
        s0 = (jnp.zeros((batch * ML_HEADS, ML_DK, 2 * ML_DV), F32),
              jnp.zeros((batch * ML_HEADS, 8, LANES), F32))
        h_cf, c_f, m_f = _ml_scan(qk_c, p3_c, gt_c, gg_c, s0, False)
        h_lf, _, _ = _ml_scan(qk_l, p3_l, gt_l, gg_l, (c_f, m_f), False)
        if need_ctx:
            ml_c, c_b, m_b = _ml_scan(qk_c, p3_c, gt_c, gg_c, s0, True, (h_cf, ml_gain))
        else:
            _, c_b, m_b = _ml_scan(qk_c, p3_c, gt_c, gg_c, s0, True)
        ml_l, _, _ = _ml_scan(qk_l, p3_l, gt_l, gg_l, (c_b, m_b), True, (h_lf, ml_gain))

        h_l = _out_proj(da_l, ml_l.reshape(batch * seq, -1), mla_l, w_out_l, xl, mod, seq, 0)
        xl = _ffn(h_l, mod, g2, w_gu_l, w_down_l, seq, 0)
        if need_ctx:
            da_c = _attention("da", qd_c, [(kd_c, p_c, C_DAV)], batch, [lam_p, da_gain], lam_init)
            mla_c = _attention("mla", qm_c, [(km_c, vm_c, 0)], batch, [])
            h_c = _out_proj(da_c, ml_c.reshape(batch * ctx_len, -1), mla_c, w_out_l, xc, mod, huge, batch)
            xc = _ffn(h_c, mod, g2, w_gu_l, w_down_l, huge, batch)

    return xl.reshape(batch, seq, d).astype(in_dtype)
```

```python
import functools
import math

import jax
import jax.numpy as jnp
from jax import lax
from jax.experimental import pallas as pl
from jax.experimental.pallas import tpu as pltpu

F32 = jnp.float32
BF16 = jnp.bfloat16

DA_HEADS = 4
DA_HALF = 64
DA_V = 2 * DA_HALF
ML_HEADS = 4
ML_DK = 128
ML_DV = 128
MLA_HEADS = 8
MLA_NOPE = 128
MLA_ROPE = 64
MLA_V = 128
MLA_QK = MLA_NOPE + MLA_ROPE
MLA_PAD = 256
GRID_W = 64
ROPE_DIM = 64
ROPE_BASE = 10000.0
EPS = 1e-6
LOG2E = 1.4426950408889634

LANES = 128
ML_CHUNK = 128
VMEM_LIMIT = 56 * 1024 * 1024

C_DAQ, C_DAK, C_DAV = 0, 512, 1024
C_MLQ, C_MLK, C_MLV, C_MLO = 1536, 2048, 2560, 3072
C_CQ, C_CKV, C_KPE, C_GATE = 3584, 4096, 4352, 4480
P_COLS = 4608


def _cparams(*sem):
    return pltpu.CompilerParams(dimension_semantics=sem, vmem_limit_bytes=VMEM_LIMIT)


def _pick(n, pref):
    if n <= pref:
        return n
    t = pref
    while n % t:
        t //= 2
    return t


def _mod_kernel(c_ref, w_ref, b_ref, o_ref):
    c = c_ref[...]
    s = (c / (1.0 + jnp.exp(-c))).astype(BF16)
    o_ref[...] = jnp.dot(s, w_ref[...].astype(BF16), preferred_element_type=F32) + b_ref[...]


def _modulation(cc, mod_w, mod_b):
    depth, d, n = mod_w.shape
    tn = _pick(n, 1024)
    return pl.pallas_call(
        _mod_kernel,
        out_shape=jax.ShapeDtypeStruct((depth, cc.shape[0], n), F32),
        grid=(depth, n // tn),
        in_specs=[
            pl.BlockSpec((cc.shape[0], d), lambda l, j: (0, 0)),
            pl.BlockSpec((None, d, tn), lambda l, j: (l, 0, j)),
            pl.BlockSpec((None, 1, tn), lambda l, j: (l, 0, j)),
        ],
        out_specs=pl.BlockSpec((None, cc.shape[0], tn), lambda l, j: (l, 0, j)),
        compiler_params=_cparams("arbitrary", "arbitrary"),
        name="modulation",
    )(cc, mod_w, mod_b.reshape(depth, 1, n))


def _norm_rows(x_ref, g_ref, shift, scale1, out_ref, rc):
    tm = x_ref.shape[0]
    g = g_ref[...]

    def body(r, carry):
        rows = pl.ds(pl.multiple_of(r * rc, rc), rc)
        x = x_ref[rows, :]
        ms = jnp.mean(x * x, axis=-1, keepdims=True)
        y = x * lax.rsqrt(ms + EPS) * g
        out_ref[rows, :] = (y * scale1 + shift).astype(out_ref.dtype)
        return carry

    lax.fori_loop(0, tm // rc, body, 0)


def _in_proj_kernel(x_ref, mod_ref, g_ref, w_ref, p_ref, gate_ref, xn_ref, *, d, rc):
    n = pl.program_id(1)

    @pl.when(n == 0)
    def _():
        shift = mod_ref[:, 0:d]
        scale1 = 1.0 + mod_ref[:, d:2 * d]
        _norm_rows(x_ref, g_ref, shift, scale1, xn_ref, rc)

    acc = jnp.dot(xn_ref[...], w_ref[...], preferred_element_type=F32)
    p_ref[...] = acc.astype(p_ref.dtype)

    @pl.when(n == pl.num_programs(1) - 1)
    def _():
        gate_ref[...] = acc[:, acc.shape[1] - LANES:]


def _in_proj(x, mod, g, w, layer, rows_per_mod, mod_base):
    r, d = x.shape
    n = w.shape[2]
    tm = _pick(min(r, rows_per_mod), 1024)
    tn = 768
    assert n % tn == 0
    return pl.pallas_call(
        functools.partial(_in_proj_kernel, d=d, rc=_pick(tm, 64)),
        out_shape=(jax.ShapeDtypeStruct((r, n), BF16), jax.ShapeDtypeStruct((r, LANES), F32)),
        grid=(r // tm, n // tn),
        in_specs=[
            pl.BlockSpec((tm, d), lambda i, j: (i, 0)),
            pl.BlockSpec((None, 1, 6 * d), lambda i, j: (i * tm // rows_per_mod + mod_base, 0, 0)),
            pl.BlockSpec((1, d), lambda i, j: (0, 0)),
            pl.BlockSpec((None, d, tn), lambda i, j: (layer, 0, j)),
        ],
        out_specs=(
            pl.BlockSpec((tm, tn), lambda i, j: (i, j)),
            pl.BlockSpec((tm, LANES), lambda i, j: (i, 0)),
        ),
        scratch_shapes=[pltpu.VMEM((tm, d), BF16)],
        compiler_params=_cparams("arbitrary", "arbitrary"),
        name="in_proj",
    )(x, mod, g, w)


def _swap16(x):
    lane = lax.broadcasted_iota(jnp.int32, x.shape, 1)
    fwd = pltpu.roll(x, LANES - 16, 1)
    bwd = pltpu.roll(x, 16, 1)
    return jnp.where((lane % 32) < 16, fwd, bwd)


def _rope(x, cos, sin):
    return x * cos + _swap16(x) * sin


def _split3(x):
    hi = x.astype(BF16)
    r1 = x - hi.astype(F32)
    mid = r1.astype(BF16)
    lo = (r1 - mid.astype(F32)).astype(BF16)
    return hi, mid, lo


def _group_mean_sq(x, gm):
    return sum(jnp.dot(part, gm, preferred_element_type=F32) for part in _split3(x * x))


def _da_prep_kernel(q_ref, k_ref, cos_ref, sin_ref, gq_ref, gk_ref, gm_ref, qo_ref, ko_ref, *, qscale):
    cos = cos_ref[...]
    sin = sin_ref[...]
    gm = gm_ref[...]
    lane = lax.broadcasted_iota(jnp.int32, cos.shape, 1)
    first = lane < DA_HALF
    for h in range(DA_HEADS):
        sl = slice(h * LANES, (h + 1) * LANES)
        q = q_ref[:, sl].astype(F32)
        q = q * lax.rsqrt(_group_mean_sq(q, gm) + EPS) * gq_ref[...]
        q = _rope(q, cos, sin) * qscale
        qo_ref[:, 2 * h * LANES:(2 * h + 1) * LANES] = jnp.where(first, q, 0.0).astype(qo_ref.dtype)
        qo_ref[:, (2 * h + 1) * LANES:(2 * h + 2) * LANES] = jnp.where(first, 0.0, q).astype(qo_ref.dtype)
        k = k_ref[:, sl].astype(F32)
        k = k * lax.rsqrt(_group_mean_sq(k, gm) + EPS) * gk_ref[...]
        ko_ref[:, sl] = _rope(k, cos, sin).astype(ko_ref.dtype)


def _da_prep(p, cos, sin, gq, gk, gm):
    r = p.shape[0]
    tm = _pick(r, 256)
    tab_blocks = cos.shape[0] // tm
    w = DA_HEADS * LANES
    qscale = DA_HALF ** -0.5 * LOG2E
    return pl.pallas_call(
        functools.partial(_da_prep_kernel, qscale=qscale),
        out_shape=(jax.ShapeDtypeStruct((r, 2 * w), BF16), jax.ShapeDtypeStruct((r, w), BF16)),
        grid=(r // tm,),
        in_specs=[
            pl.BlockSpec((tm, w), lambda i: (i, C_DAQ // w)),
            pl.BlockSpec((tm, w), lambda i: (i, C_DAK // w)),
            pl.BlockSpec((tm, LANES), lambda i: (i % tab_blocks, 0)),
            pl.BlockSpec((tm, LANES), lambda i: (i % tab_blocks, 0)),
            pl.BlockSpec((1, LANES), lambda i: (0, 0)),
            pl.BlockSpec((1, LANES), lambda i: (0, 0)),
            pl.BlockSpec((LANES, LANES), lambda i: (0, 0)),
        ],
        out_specs=(
            pl.BlockSpec((tm, 2 * w), lambda i: (i, 0)),
            pl.BlockSpec((tm, w), lambda i: (i, 0)),
        ),
        compiler_params=_cparams("arbitrary"),
        name="da_prep",
    )(p, p, cos, sin, gq, gk, gm)


def _mla_prep_kernel(cq_ref, ckv_ref, kpe_ref, cos_ref, sin_ref, gcq_ref, gckv_ref, gq_ref, gk_ref,
                     wq_ref, wkv_ref, qo_ref, ko_ref, vo_ref, *, qscale):
    cos = cos_ref[...]
    sin = sin_ref[...]
    lane = lax.broadcasted_iota(jnp.int32, cos.shape, 1)
    inv_qk = 1.0 / MLA_QK

    cq = cq_ref[...].astype(F32)
    cq = cq * lax.rsqrt(jnp.mean(cq * cq, axis=-1, keepdims=True) + EPS) * gcq_ref[...]
    qf = jnp.dot(cq.astype(BF16), wq_ref[...], preferred_element_type=F32)

    ckv = ckv_ref[...].astype(F32)
    ckv = ckv * lax.rsqrt(jnp.mean(ckv * ckv, axis=-1, keepdims=True) + EPS) * gckv_ref[...]
    kvf = jnp.dot(ckv.astype(BF16), wkv_ref[...], preferred_element_type=F32)

    kpe = jnp.where(lane < MLA_ROPE, kpe_ref[...].astype(F32), 0.0)
    kpe_ss = jnp.sum(kpe * kpe, axis=-1, keepdims=True)
    kpe_rot = _rope(kpe * gk_ref[:, LANES:], cos, sin)

    for h in range(MLA_HEADS):
        q1 = qf[:, h * MLA_PAD:h * MLA_PAD + LANES]
        q2 = qf[:, h * MLA_PAD + LANES:(h + 1) * MLA_PAD]
        ms = (jnp.sum(q1 * q1, axis=-1, keepdims=True) + jnp.sum(q2 * q2, axis=-1, keepdims=True)) * inv_qk
        rq = lax.rsqrt(ms + EPS) * qscale
        qo_ref[:, h * MLA_PAD:h * MLA_PAD + LANES] = (q1 * rq * gq_ref[:, :LANES]).astype(qo_ref.dtype)
        q2 = _rope(q2 * rq * gq_ref[:, LANES:], cos, sin)
        qo_ref[:, h * MLA_PAD + LANES:(h + 1) * MLA_PAD] = q2.astype(qo_ref.dtype)

        kn = kvf[:, h * MLA_PAD:h * MLA_PAD + LANES]
        ms = (jnp.sum(kn * kn, axis=-1, keepdims=True) + kpe_ss) * inv_qk
        rk = lax.rsqrt(ms + EPS)
        ko_ref[:, h * MLA_PAD:h * MLA_PAD + LANES] = (kn * rk * gk_ref[:, :LANES]).astype(ko_ref.dtype)
        ko_ref[:, h * MLA_PAD + LANES:(h + 1) * MLA_PAD] = (kpe_rot * rk).astype(ko_ref.dtype)
        vo_ref[:, h * MLA_V:(h + 1) * MLA_V] = kvf[:, h * MLA_PAD + LANES:(h + 1) * MLA_PAD].astype(vo_ref.dtype)


def _mla_prep(p, cos, sin, gcq, gckv, gq, gk, wq, wkv, layer):
    r = p.shape[0]
    tm = _pick(r, 256)
    tab_blocks = cos.shape[0] // tm
    q_rank = wq.shape[1]
    kv_rank = wkv.shape[1]
    wide = MLA_HEADS * MLA_PAD
    qscale = MLA_QK ** -0.5 * LOG2E
    const = lambda i: (0, 0)
    return pl.pallas_call(
        functools.partial(_mla_prep_kernel, qscale=qscale),
        out_shape=(jax.ShapeDtypeStruct((r, wide), BF16), jax.ShapeDtypeStruct((r, wide), BF16),
                   jax.ShapeDtypeStruct((r, MLA_HEADS * MLA_V), BF16)),
        grid=(r // tm,),
        in_specs=[
            pl.BlockSpec((tm, q_rank), lambda i: (i, C_CQ // q_rank)),
            pl.BlockSpec((tm, kv_rank), lambda i: (i, C_CKV // kv_rank)),
            pl.BlockSpec((tm, LANES), lambda i: (i, C_KPE // LANES)),
            pl.BlockSpec((tm, LANES), lambda i: (i % tab_blocks, 0)),
            pl.BlockSpec((tm, LANES), lambda i: (i % tab_blocks, 0)),
            pl.BlockSpec((1, q_rank), const),
            pl.BlockSpec((1, kv_rank), const),
            pl.BlockSpec((1, MLA_PAD), const),
            pl.BlockSpec((1, MLA_PAD), const),
            pl.BlockSpec((None, q_rank, wide), lambda i: (layer, 0, 0)),
            pl.BlockSpec((None, kv_rank, wide), lambda i: (layer, 0, 0)),
        ],
        out_specs=(
            pl.BlockSpec((tm, wide), lambda i: (i, 0)),
            pl.BlockSpec((tm, wide), lambda i: (i, 0)),
            pl.BlockSpec((tm, MLA_HEADS * MLA_V), lambda i: (i, 0)),
        ),
        compiler_params=_cparams("arbitrary"),
        name="mla_prep",
    )(p, p, p, cos, sin, gcq, gckv, gq, gk, wq, wkv)


def _attn_core(q_ref, k_refs, v_refs, n_comp, dq, kc):
    chunks = []
    for ri, kr in enumerate(k_refs):
        rows = kr.shape[0]
        step = min(kc, rows)
        for r0 in range(0, rows, step):
            chunks.append((ri, r0, step))
    outs = []
    for c in range(n_comp):
        q = q_ref[:, c * dq:(c + 1) * dq]
        m = acc = None
        for ri, r0, step in chunks:
            s = lax.dot_general(q, k_refs[ri][r0:r0 + step, :], (((1,), (1,)), ((), ())),
                                preferred_element_type=F32)
            cm = jnp.max(s, axis=-1, keepdims=True)
            m_new = cm if m is None else jnp.maximum(m, cm)
            p = jnp.exp2(s - m_new)
            vc = v_refs[ri][r0:r0 + step, :]
            pv = jnp.dot(p.astype(BF16), jnp.concatenate([vc, jnp.ones_like(vc)], axis=-1),
                         preferred_element_type=F32)
            acc = pv if m is None else jnp.exp2(m - m_new) * acc + pv
            m = m_new
        dv = acc.shape[1] // 2
        outs.append(acc[:, :dv] / acc[:, dv:])
    return outs


def _da_attn_kernel(*refs, n_kv, lam_init, kc):
    q_ref = refs[0]
    k_refs = refs[1:1 + n_kv]
    v_refs = refs[1 + n_kv:1 + 2 * n_kv]
    lam_ref, g_ref, o_ref = refs[1 + 2 * n_kv:]
    lp = lam_ref[...]
    lam = (jnp.exp(jnp.sum(lp[0:1] * lp[1:2], axis=-1, keepdims=True))
           - jnp.exp(jnp.sum(lp[2:3] * lp[3:4], axis=-1, keepdims=True)) + lam_init)
    o0, o1 = _attn_core(q_ref, k_refs, v_refs, 2, LANES, kc)
    o = o0 - lam * o1
    o = o * lax.rsqrt(jnp.mean(o * o, axis=-1, keepdims=True) + EPS) * g_ref[...]
    o_ref[...] = (o * (1.0 - lam_init)).astype(o_ref.dtype)


def _mla_attn_kernel(*refs, n_kv, kc):
    q_ref = refs[0]
    k_refs = refs[1:1 + n_kv]
    v_refs = refs[1 + n_kv:1 + 2 * n_kv]
    (o_ref,) = refs[1 + 2 * n_kv:]
    (o,) = _attn_core(q_ref, k_refs, v_refs, 1, MLA_PAD, kc)
    o_ref[...] = o.astype(o_ref.dtype)


def _attention(kind, q, kvs, batch, extra, lam_init=0.0):
    heads = DA_HEADS if kind == "da" else MLA_HEADS
    qw = 2 * LANES if kind == "da" else MLA_PAD
    dq = LANES if kind == "da" else MLA_PAD
    dv = DA_V if kind == "da" else MLA_V
    rq = q.shape[0]
    tq = _pick(rq // batch, 1024)
    nq = rq // batch // tq
    kc = 1024
    k_specs, v_specs, k_args, v_args = [], [], [], []
    for k, v, vcol in kvs:
        kl = k.shape[0] // batch
        k_specs.append(pl.BlockSpec((kl, dq), lambda b, h, i: (b, h)))
        v_specs.append(pl.BlockSpec((kl, dv), lambda b, h, i, vcol=vcol: (b, vcol // dv + h)))
        k_args.append(k)
        v_args.append(v)
    extra_specs = [pl.BlockSpec(e.shape, lambda b, h, i: (0, 0)) for e in extra]
    if kind == "da":
        body = functools.partial(_da_attn_kernel, n_kv=len(kvs), lam_init=lam_init, kc=kc)
    else:
        body = functools.partial(_mla_attn_kernel, n_kv=len(kvs), kc=kc)
    return pl.pallas_call(
        body,
        out_shape=jax.ShapeDtypeStruct((rq, heads * dv), BF16),
        grid=(batch, heads, nq),
        in_specs=[pl.BlockSpec((tq, qw), lambda b, h, i: (b * nq + i, h))] + k_specs + v_specs + extra_specs,
        out_specs=pl.BlockSpec((tq, dv), lambda b, h, i: (b * nq + i, h)),
        compiler_params=_cparams("arbitrary", "arbitrary", "arbitrary"),
        name=kind + "_attn",
    )(q, *k_args, *v_args, *extra)


def _ml_conv_kernel(xq_ref, xk_ref, wq_ref, wk_ref, bq_ref, bk_ref, q_ref, kt_ref, *, kscale, tc):
    s = xq_ref.shape[0]
    t = lax.broadcasted_iota(jnp.int32, xq_ref.shape, 0)

    def conv_silu(x_ref, w_ref, b_ref):
        x = x_ref[...].astype(F32)
        prev = jnp.where(t == 0, 0.0, pltpu.roll(x, 1, 0))
        nxt = jnp.where(t == s - 1, 0.0, pltpu.roll(x, s - 1, 0))
        y = b_ref[...] + prev * w_ref[0:1, :] + x * w_ref[1:2, :] + nxt * w_ref[2:3, :]
        return y / (1.0 + jnp.exp(-y))

    q_ref[...] = conv_silu(xq_ref, wq_ref, bq_ref).astype(q_ref.dtype)
    k = conv_silu(xk_ref, wk_ref, bk_ref) * kscale
    for r0 in range(0, s, tc):
        kt_ref[:, r0:r0 + tc] = k[r0:r0 + tc, :].T.astype(kt_ref.dtype)


def _ml_conv(p3, conv_w, conv_b):
    nseq, s, _ = p3.shape
    w = ML_HEADS * ML_DK
    q0 = C_MLQ // LANES
    k0 = C_MLK // LANES
    return pl.pallas_call(
        functools.partial(_ml_conv_kernel, kscale=ML_DK ** -0.5, tc=_pick(s, 512)),
        out_shape=(jax.ShapeDtypeStruct((nseq, s, w), BF16), jax.ShapeDtypeStruct((nseq, w, s), BF16)),
        grid=(nseq, ML_HEADS),
        in_specs=[
            pl.BlockSpec((None, s, LANES), lambda b, h: (b, 0, q0 + h)),
            pl.BlockSpec((None, s, LANES), lambda b, h: (b, 0, k0 + h)),
            pl.BlockSpec((3, LANES), lambda b, h: (0, h)),
            pl.BlockSpec((3, LANES), lambda b, h: (0, ML_HEADS + h)),
            pl.BlockSpec((1, LANES), lambda b, h: (0, h)),
            pl.BlockSpec((1, LANES), lambda b, h: (0, ML_HEADS + h)),
        ],
        out_specs=(
            pl.BlockSpec((None, s, LANES), lambda b, h: (b, 0, h)),
            pl.BlockSpec((None, ML_DK, s), lambda b, h: (b, h, 0)),
        ),
        compiler_params=_cparams("arbitrary", "arbitrary"),
        name="ml_conv",
    )(p3, p3, conv_w, conv_w, conv_b, conv_b)


def _ml_gate_kernel(g_ref, b_ref, gt_ref, gg_ref, *, chunk):
    s = g_ref.shape[0]
    row = lax.broadcasted_iota(jnp.int32, (chunk, chunk), 0)
    col = lax.broadcasted_iota(jnp.int32, (chunk, chunk), 1)
    prefix = jnp.where(col <= row, 1.0, 0.0).astype(BF16)
    suffix = jnp.where(col >= row, 1.0, 0.0).astype(BF16)
    lane = lax.broadcasted_iota(jnp.int32, (chunk, LANES), 1)
    is_forget = ((lane % (2 * ML_HEADS)) >= ML_HEADS) & (lane < 4 * ML_HEADS)
    is_fwd = lane < 2 * ML_HEADS
    for r0 in range(0, s, chunk):
        g = g_ref[r0:r0 + chunk, :] + b_ref[...]
        lf = jnp.minimum(g, 0.0) - jnp.log(1.0 + jnp.exp(-jnp.abs(g)))
        parts = _split3(jnp.where(is_forget, lf, 0.0))
        cum_f = sum(jnp.dot(prefix, part, preferred_element_type=F32) for part in parts)
        cum_b = sum(jnp.dot(suffix, part, preferred_element_type=F32) for part in parts)
        out = jnp.where(is_forget, jnp.where(is_fwd, cum_f, cum_b), g)
        gt_ref[r0:r0 + chunk, :] = out
        gg_ref[:, r0:r0 + chunk] = out.T[0:4 * ML_HEADS, :]


def _ml_gates(g3, gate_b):
    nseq, s, _ = g3.shape
    return pl.pallas_call(
        functools.partial(_ml_gate_kernel, chunk=min(ML_CHUNK, s)),
        out_shape=(jax.ShapeDtypeStruct((nseq, s, LANES), F32),
                   jax.ShapeDtypeStruct((nseq, 4 * ML_HEADS, s), F32)),
        grid=(nseq,),
        in_specs=[
            pl.BlockSpec((None, s, LANES), lambda b: (b, 0, 0)),
            pl.BlockSpec((1, LANES), lambda b: (0, 0)),
        ],
        out_specs=(
            pl.BlockSpec((None, s, LANES), lambda b: (b, 0, 0)),
            pl.BlockSpec((None, 4 * ML_HEADS, s), lambda b: (b, 0, 0)),
        ),
        compiler_params=_cparams("arbitrary"),
        name="ml_gates",
    )(g3, gate_b)


def _ml_scan_kernel(*refs, nb, reverse, post):
    if post:
        (q_ref, kt_ref, v_ref, gt_ref, gg_ref, c0_ref, m0_ref, hp_ref, og_ref, gain_ref,
         h_ref, c_out_ref, m_out_ref, c_sc, m_sc) = refs
    else:
        (q_ref, kt_ref, v_ref, gt_ref, gg_ref, c0_ref, m0_ref,
         h_ref, c_out_ref, m_out_ref, c_sc, m_sc) = refs
    step = pl.program_id(0)
    L = q_ref.shape[1]
    nh = ML_HEADS

    @pl.when(step == 0)
    def _():
        c_sc[...] = c0_ref[...]
        m_sc[...] = m0_ref[...]

    row = lax.broadcasted_iota(jnp.int32, (L, L), 0)
    col = lax.broadcasted_iota(jnp.int32, (L, L), 1)
    incl = (col >= row) if reverse else (col <= row)
    last = 0 if reverse else L - 1

    def lanes(h):
        ig_lane = (2 * nh if reverse else 0) + h
        return ig_lane, ig_lane + nh

    col_rep = {}
    for b in range(nb):
        gt = gt_ref[b]
        for h in range(nh):
            ig_lane, lf_lane = lanes(h)
            col_rep[b, h] = (jnp.broadcast_to(gt[:, lf_lane:lf_lane + 1], (L, LANES)),
                             jnp.broadcast_to(gt[:, ig_lane:ig_lane + 1], (L, LANES)))

    for b in range(nb):
        gg = gg_ref[b]
        for h in range(nh):
            bh = b * nh + h
            ig_lane, lf_lane = lanes(h)
            hs = slice(h * ML_DK, (h + 1) * ML_DK)
            q = q_ref[b, :, hs]
            kt = kt_ref[b, hs, :]
            v = v_ref[b, :, hs]
            c_aug = c_sc[bh]
            m_prev = m_sc[bh][0:1, :]

            bcum, ig = col_rep[b, h]
            bcum_row = gg[lf_lane:lf_lane + 1, :]
            ig_row = gg[ig_lane:ig_lane + 1, :]
            total = bcum[last:last + 1, :]

            log_d = jnp.where(incl, bcum[:, :L] - bcum_row + ig_row, -jnp.inf)
            log_prev = bcum + m_prev
            m_t = jnp.maximum(log_prev, jnp.max(log_d, axis=-1, keepdims=True))
            dmat = jnp.exp(log_d - m_t[:, :L])
            w_prev = jnp.exp(log_prev - m_t)
            s = jnp.dot(q, kt, preferred_element_type=F32) * dmat
            lhs = jnp.concatenate([s.astype(BF16), (w_prev * q.astype(F32)).astype(BF16)], axis=-1)
            rhs = jnp.concatenate([jnp.concatenate([v, jnp.ones_like(v)], axis=-1), c_aug.astype(BF16)], axis=0)
            tot = jnp.dot(lhs, rhs, preferred_element_type=F32)
            num = tot[:, :ML_DV]
            den = tot[:, ML_DV:]
            hv = num / jnp.maximum(jnp.abs(den), jnp.exp(-m_t))

            m_new = m_t[last:last + 1, :]
            w_s = jnp.exp(total - bcum + ig - m_new)
            decay = jnp.exp(total + m_prev - m_new)
            wv = jnp.concatenate([w_s * v.astype(F32), w_s], axis=-1).astype(BF16)
            kv = jnp.dot(kt, wv, preferred_element_type=F32)
            c_sc[bh] = jnp.concatenate([decay, decay], axis=-1) * c_aug + kv
            m_sc[bh] = jnp.broadcast_to(m_new, m_sc.shape[1:])

            if post:
                hv = hv + hp_ref[b, :, hs]
                hv = hv * lax.rsqrt(jnp.mean(hv * hv, axis=-1, keepdims=True) + EPS) * gain_ref[:, hs]
                og = og_ref[b, :, hs].astype(F32)
                hv = hv / (1.0 + jnp.exp(-og))
            h_ref[b, :, hs] = hv.astype(h_ref.dtype)

    @pl.when(step == pl.num_programs(0) - 1)
    def _():
        c_out_ref[...] = c_sc[...]
        m_out_ref[...] = m_sc[...]


def _ml_scan(q3, kt3, p3, gt3, gg3, state, reverse, post_args=None):
    nb, s, _ = q3.shape
    assert ML_DK == LANES and ML_DV == LANES and min(ML_CHUNK, s) <= LANES
    L = min(ML_CHUNK, s)
    nc = s // L
    w = ML_HEADS * ML_DK
    c0, m0 = state
    post = post_args is not None
    cidx = (lambda c: nc - 1 - c) if reverse else (lambda c: c)
    in_specs = [
        pl.BlockSpec((nb, L, w), lambda c: (0, cidx(c), 0)),
        pl.BlockSpec((nb, w, L), lambda c: (0, 0, cidx(c))),
        pl.BlockSpec((nb, L, w), lambda c: (0, cidx(c), C_MLV // w)),
        pl.BlockSpec((nb, L, LANES), lambda c: (0, cidx(c), 0)),
        pl.BlockSpec((nb, 4 * ML_HEADS, L), lambda c: (0, 0, cidx(c))),
        pl.BlockSpec(c0.shape, lambda c: (0, 0, 0)),
        pl.BlockSpec(m0.shape, lambda c: (0, 0, 0)),
    ]
    args = [q3, kt3, p3, gt3, gg3, c0, m0]
    if post:
        h_prev, gain = post_args
        in_specs += [
            pl.BlockSpec((nb, L, w), lambda c: (0, cidx(c), 0)),
            pl.BlockSpec((nb, L, w), lambda c: (0, cidx(c), C_MLO // w)),
            pl.BlockSpec((1, w), lambda c: (0, 0)),
        ]
        args += [h_prev, p3, gain]
    return pl.pallas_call(
        functools.partial(_ml_scan_kernel, nb=nb, reverse=reverse, post=post),
        out_shape=(jax.ShapeDtypeStruct((nb, s, w), BF16 if post else F32),
                   jax.ShapeDtypeStruct(c0.shape, F32), jax.ShapeDtypeStruct(m0.shape, F32)),
        grid=(nc,),
        in_specs=in_specs,
        out_specs=(
            pl.BlockSpec((nb, L, w), lambda c: (0, cidx(c), 0)),
            pl.BlockSpec(c0.shape, lambda c: (0, 0, 0)),
            pl.BlockSpec(m0.shape, lambda c: (0, 0, 0)),
        ),
        scratch_shapes=[pltpu.VMEM(c0.shape, F32), pltpu.VMEM(m0.shape, F32)],
        compiler_params=_cparams("arbitrary"),
        name="ml_scan",
    )(*args)


def _out_proj_kernel(da_ref, ml_ref, mla_ref, w_ref, x_ref, gate_ref, h_ref):
    n_da = da_ref.shape[1]
    n_ml = ml_ref.shape[1]
    acc = jnp.dot(da_ref[...], w_ref[0:n_da, :], preferred_element_type=F32)
    acc += jnp.dot(ml_ref[...], w_ref[n_da:n_da + n_ml, :], preferred_element_type=F32)
    acc += jnp.dot(mla_ref[...], w_ref[n_da + n_ml:, :], preferred_element_type=F32)
    h_ref[...] = x_ref[...] + gate_ref[...] * acc


def _out_proj(da, ml, mla, w, layer, x, mod, rows_per_mod, mod_base):
    r, d = x.shape
    tm = _pick(min(r, rows_per_mod), 1024)
    tn = _pick(d, 1024)
    nt = d // tn
    return pl.pallas_call(
        _out_proj_kernel,
        out_shape=jax.ShapeDtypeStruct((r, d), F32),
        grid=(nt, r // tm),
        in_specs=[
            pl.BlockSpec((tm, da.shape[1]), lambda j, i: (i, 0)),
            pl.BlockSpec((tm, ml.shape[1]), lambda j, i: (i, 0)),
            pl.BlockSpec((tm, mla.shape[1]), lambda j, i: (i, 0)),
            pl.BlockSpec((None, w.shape[1], tn), lambda j, i: (layer, 0, j)),
            pl.BlockSpec((tm, tn), lambda j, i: (i, j)),
            pl.BlockSpec((None, 1, tn), lambda j, i: (i * tm // rows_per_mod + mod_base, 0, 2 * nt + j)),
        ],
        out_specs=pl.BlockSpec((tm, tn), lambda j, i: (i, j)),
        compiler_params=_cparams("arbitrary", "arbitrary"),
        name="out_proj",
    )(da, ml, mla, w, x, mod)


def _ffn_kernel(h_ref, mod_ref, g_ref, wg_ref, wu_ref, wd_ref, o_ref, hn_ref, *, d, rc):
    j = pl.program_id(1)

    @pl.when(j == 0)
    def _():
        shift = mod_ref[:, 3 * d:4 * d]
        scale1 = 1.0 + mod_ref[:, 4 * d:5 * d]
        _norm_rows(h_ref, g_ref, shift, scale1, hn_ref, rc)
        o_ref[...] = h_ref[...]

    hn = hn_ref[...]
    gate = jnp.dot(hn, wg_ref[...], preferred_element_type=F32)
    up = jnp.dot(hn, wu_ref[...], preferred_element_type=F32)
    act = (gate / (1.0 + jnp.exp(-gate)) * up).astype(BF16)
    o_ref[...] += mod_ref[:, 5 * d:6 * d] * jnp.dot(act, wd_ref[...], preferred_element_type=F32)


def _ffn(h, mod, g, w_gu, w_down, layer, rows_per_mod, mod_base):
    r, d = h.shape
    ff = w_down.shape[1]
    tm = _pick(min(r, rows_per_mod), 512)
    th = 512 if ff % 512 == 0 else 256
    nh = ff // th
    return pl.pallas_call(
        functools.partial(_ffn_kernel, d=d, rc=_pick(tm, 64)),
        out_shape=jax.ShapeDtypeStruct((r, d), F32),
        grid=(r // tm, nh),
        in_specs=[
            pl.BlockSpec((tm, d), lambda i, j: (i, 0)),
            pl.BlockSpec((None, 1, 6 * d), lambda i, j: (i * tm // rows_per_mod + mod_base, 0, 0)),
            pl.BlockSpec((1, d), lambda i, j: (0, 0)),
            pl.BlockSpec((None, d, th), lambda i, j: (layer, 0, j)),
            pl.BlockSpec((None, d, th), lambda i, j: (layer, 0, nh + j)),
            pl.BlockSpec((None, th, d), lambda i, j: (layer, j, 0)),
        ],
        out_specs=pl.BlockSpec((tm, d), lambda i, j: (i, 0)),
        scratch_shapes=[pltpu.VMEM((tm, d), BF16)],
        compiler_params=_cparams("arbitrary", "arbitrary"),
        name="ffn",
    )(h, mod, g, w_gu, w_gu, w_down)


def _rope_tables(n_lat):
    pos = jnp.arange(n_lat, dtype=jnp.int32)
    r = (pos // GRID_W).astype(F32)
    c = (pos % GRID_W).astype(F32)
    half = ROPE_DIM // 4
    inv = ROPE_BASE ** (-jnp.arange(half, dtype=F32) / half)
    ar = r[:, None] * inv
    ac = c[:, None] * inv
    cos64 = jnp.concatenate([jnp.cos(ar), jnp.cos(ar), jnp.cos(ac), jnp.cos(ac)], axis=-1)
    sin64 = jnp.concatenate([-jnp.sin(ar), jnp.sin(ar), -jnp.sin(ac), jnp.sin(ac)], axis=-1)
    return cos64, sin64


def _pad_cols(a, width):
    return jnp.pad(a, ((0, 0),) * (a.ndim - 1) + ((0, width - a.shape[-1]),))


def _layout_w_in(w):
    gates0 = C_CQ
    gates1 = gates0 + 4 * ML_HEADS
    main = w[..., :gates0]
    gates = w[..., gates0:gates1]
    rest = w[..., gates1:]
    n_cq_ckv = C_KPE - C_CQ
    cq_ckv = rest[..., :n_cq_ckv]
    kpe = rest[..., n_cq_ckv:]
    return jnp.concatenate([main, cq_ckv, _pad_cols(kpe, LANES), _pad_cols(gates, LANES)], axis=-1)


def kernel(x, c, ctx, c_ctx, mod_w, mod_b, norm1_g, norm2_g, w_in, da_qk_g, da_lambda, da_out_g,
           ml_conv_w, ml_conv_b, ml_gate_b, ml_out_g, mla_q_norm_g, mla_kv_norm_g, mla_w_uq, mla_w_ukv,
           mla_qk_g, w_out, ffn_w_gu, ffn_w_down):
    batch, seq, d = x.shape
    ctx_len = ctx.shape[1]
    depth = mod_w.shape[0]
    in_dtype = x.dtype
    assert w_in.shape[2] == C_CQ + 4 * ML_HEADS + (C_KPE - C_CQ) + MLA_ROPE

    n_mod_rows = -(-(batch + 1) // 8) * 8
    cc = jnp.concatenate([c, c_ctx[None, :], jnp.zeros((n_mod_rows - batch - 1, d), F32)], axis=0)
    mod_all = _modulation(cc, mod_w, mod_b)

    cos64, sin64 = _rope_tables(seq)
    cos_da = jnp.tile(cos64, (1, 2))
    sin_da = jnp.tile(sin64, (1, 2))
    cos_mla = jnp.concatenate([cos64, jnp.ones_like(cos64)], axis=-1)
    sin_mla = jnp.concatenate([sin64, jnp.zeros_like(sin64)], axis=-1)
    tab_rows = _pick(batch * ctx_len, 256)
    cos_id = jnp.ones((tab_rows, LANES), F32)
    sin_id = jnp.zeros((tab_rows, LANES), F32)
    lane = jnp.arange(LANES)
    group_mean = jnp.where((lane[:, None] // DA_HALF) == (lane[None, :] // DA_HALF),
                           1.0 / DA_HALF, 0.0).astype(BF16)

    w_in_b = _layout_w_in(w_in).astype(BF16)
    w_out_b = w_out.astype(BF16)
    w_gu_b = ffn_w_gu.astype(BF16)
    w_down_b = ffn_w_down.astype(BF16)
    wq_b = jnp.pad(mla_w_uq.reshape(depth, -1, MLA_HEADS, MLA_QK),
                   ((0, 0), (0, 0), (0, 0), (0, MLA_PAD - MLA_QK))).reshape(depth, -1, MLA_HEADS * MLA_PAD).astype(BF16)
    wkv_b = mla_w_ukv.astype(BF16)

    xl = x.reshape(batch * seq, d)
    xc = ctx.reshape(batch * ctx_len, d)
    huge = batch * max(seq, ctx_len) * 2

    for l in range(depth):
        need_ctx = l < depth - 1
        lam_init = 0.8 - 0.6 * math.exp(-0.3 * l)
        mod = mod_all[l].reshape(n_mod_rows, 1, 6 * d)
        g1 = norm1_g[l].reshape(1, d)
        g2 = norm2_g[l].reshape(1, d)
        gq_da = jnp.tile(da_qk_g[l, 0], 2).reshape(1, LANES)
        gk_da = jnp.tile(da_qk_g[l, 1], 2).reshape(1, LANES)
        gq_mla = _pad_cols(mla_qk_g[l, 0].reshape(1, MLA_QK), MLA_PAD)
        gk_mla = _pad_cols(mla_qk_g[l, 1].reshape(1, MLA_QK), MLA_PAD)
        gcq = mla_q_norm_g[l].reshape(1, -1)
        gckv = mla_kv_norm_g[l].reshape(1, -1)
        gate_b = _pad_cols(ml_gate_b[l].reshape(1, -1), LANES)
        conv_w = ml_conv_w[l]
        conv_b = ml_conv_b[l].reshape(1, -1)
        ml_gain = ml_out_g[l].reshape(1, -1)
        da_gain = da_out_g[l].reshape(1, -1)
        lam_p = da_lambda[l]

        p_l, g_l = _in_proj(xl, mod, g1, w_in_b, l, seq, 0)
        p_c, g_c = _in_proj(xc, mod, g1, w_in_b, l, huge, batch)

        qd_l, kd_l = _da_prep(p_l, cos_da, sin_da, gq_da, gk_da, group_mean)
        qd_c, kd_c = _da_prep(p_c, cos_id, sin_id, gq_da, gk_da, group_mean)
        da_l = _attention("da", qd_l, [(kd_l, p_l, C_DAV), (kd_c, p_c, C_DAV)], batch,
                          [lam_p, da_gain], lam_init)
        qm_l, km_l, vm_l = _mla_prep(p_l, cos_mla, sin_mla, gcq, gckv, gq_mla, gk_mla, wq_b, wkv_b, l)
        qm_c, km_c, vm_c = _mla_prep(p_c, cos_id, sin_id, gcq, gckv, gq_mla, gk_mla, wq_b, wkv_b, l)
        mla_l = _attention("mla", qm_l, [(km_l, vm_l, 0), (km_c, vm_c, 0)], batch, [])

        p3_l = p_l.reshape(batch, seq, P_COLS)
        p3_c = p_c.reshape(batch, ctx_len, P_COLS)
        q_l, kt_l = _ml_conv(p3_l, conv_w, conv_b)
        q_c, kt_c = _ml_conv(p3_c, conv_w, conv_b)
        gt_l, gg_l = _ml_gates(g_l.reshape(batch, seq, LANES), gate_b)
        gt_c, gg_c = _ml_gates(g_c.reshape(batch, ctx_len, LANES), gate_b)
        s0 = (jnp.zeros((batch * ML_HEADS, ML_DK, 2 * ML_DV), F32),
              jnp.zeros((batch * ML_HEADS, 8, LANES), F32))
        h_cf, c_f, m_f = _ml_scan(q_c, kt_c, p3_c, gt_c, gg_c, s0, False)
        h_lf, _, _ = _ml_scan(q_l, kt_l, p3_l, gt_l, gg_l, (c_f, m_f), False)
        if need_ctx:
            ml_c, c_b, m_b = _ml_scan(q_c, kt_c, p3_c, gt_c, gg_c, s0, True, (h_cf, ml_gain))
        else:
            _, c_b, m_b = _ml_scan(q_c, kt_c, p3_c, gt_c, gg_c, s0, True)
        ml_l, _, _ = _ml_scan(q_l, kt_l, p3_l, gt_l, gg_l, (c_b, m_b), True, (h_lf, ml_gain))

        h_l = _out_proj(da_l, ml_l.reshape(batch * seq, -1), mla_l, w_out_b, l, xl, mod, seq, 0)
        xl = _ffn(h_l, mod, g2, w_gu_b, w_down_b, l, seq, 0)
        if need_ctx:
            da_c = _attention("da", qd_c, [(kd_c, p_c, C_DAV)], batch, [lam_p, da_gain], lam_init)
            mla_c = _attention("mla", qm_c, [(km_c, vm_c, 0)], batch, [])
            h_c = _out_proj(da_c, ml_c.reshape(batch * ctx_len, -1), mla_c, w_out_b, l, xc, mod, huge, batch)
            xc = _ffn(h_c, mod, g2, w_gu_b, w_down_b, l, huge, batch)

    return xl.reshape(batch, seq, d).astype(in_dtype)
```

```python
import functools
import math

import jax
import jax.numpy as jnp
from jax import lax
from jax.experimental import pallas as pl
from jax.experimental.pallas import tpu as pltpu

F32 = jnp.float32
BF16 = jnp.bfloat16

DA_HEADS = 4
DA_HALF = 64
DA_V = 2 * DA_HALF
ML_HEADS = 4
ML_DK = 128
ML_DV = 128
MLA_HEADS = 8
MLA_NOPE = 128
MLA_ROPE = 64
MLA_V = 128
MLA_QK = MLA_NOPE + MLA_ROPE
MLA_PAD = 256
GRID_W = 64
ROPE_DIM = 64
ROPE_BASE = 10000.0
EPS = 1e-6
LOG2E = 1.4426950408889634

LANES = 128
ML_CHUNK = 128
VMEM_LIMIT = 56 * 1024 * 1024

C_DAQ, C_DAK, C_DAV = 0, 512, 1024
C_MLQ, C_MLK, C_MLV, C_MLO = 1536, 2048, 2560, 3072
C_CQ, C_CKV, C_KPE, C_GATE = 3584, 4096, 4352, 4480
P_COLS = 4608


def _cparams(*sem):
    return pltpu.CompilerParams(dimension_semantics=sem, vmem_limit_bytes=VMEM_LIMIT)


def _pick(n, pref):
    if n <= pref:
        return n
    t = pref
    while n % t:
        t //= 2
    return t


def _mod_kernel(c_ref, w_ref, b_ref, o_ref):
    c = c_ref[...]
    s = (c / (1.0 + jnp.exp(-c))).astype(BF16)
    o_ref[...] = jnp.dot(s, w_ref[...].astype(BF16), preferred_element_type=F32) + b_ref[...]


def _modulation(cc, mod_w, mod_b):
    depth, d, n = mod_w.shape
    tn = _pick(n, 1024)
    return pl.pallas_call(
        _mod_kernel,
        out_shape=jax.ShapeDtypeStruct((depth, cc.shape[0], n), F32),
        grid=(depth, n // tn),
        in_specs=[
            pl.BlockSpec((cc.shape[0], d), lambda l, j: (0, 0)),
            pl.BlockSpec((None, d, tn), lambda l, j: (l, 0, j)),
            pl.BlockSpec((None, 1, tn), lambda l, j: (l, 0, j)),
        ],
        out_specs=pl.BlockSpec((None, cc.shape[0], tn), lambda l, j: (l, 0, j)),
        compiler_params=_cparams("arbitrary", "arbitrary"),
        name="modulation",
    )(cc, mod_w, mod_b.reshape(depth, 1, n))


def _norm_rows(x_ref, g_ref, shift, scale1, out_ref, rc):
    tm = x_ref.shape[0]
    gs = g_ref[...] * scale1

    def body(r, carry):
        rows = pl.ds(pl.multiple_of(r * rc, rc), rc)
        x = x_ref[rows, :]
        ms = jnp.mean(x * x, axis=-1, keepdims=True)
        out_ref[rows, :] = (x * lax.rsqrt(ms + EPS) * gs + shift).astype(out_ref.dtype)
        return carry

    lax.fori_loop(0, tm // rc, body, 0, unroll=2)


def _in_proj_kernel(x_ref, mod_ref, g_ref, w_ref, p_ref, gate_ref, xn_ref, *, d, rc):
    n = pl.program_id(1)

    @pl.when(n == 0)
    def _():
        shift = mod_ref[:, 0:d]
        scale1 = 1.0 + mod_ref[:, d:2 * d]
        _norm_rows(x_ref, g_ref, shift, scale1, xn_ref, rc)

    acc = jnp.dot(xn_ref[...], w_ref[...], preferred_element_type=F32)
    p_ref[...] = acc.astype(p_ref.dtype)

    @pl.when(n == pl.num_programs(1) - 1)
    def _():
        gate_ref[...] = acc[:, acc.shape[1] - LANES:]


def _in_proj(x, mod, g, w, layer, rows_per_mod, mod_base):
    r, d = x.shape
    n = w.shape[2]
    tm = _pick(min(r, rows_per_mod), 1024)
    tn = 1536
    assert n % tn == 0
    return pl.pallas_call(
        functools.partial(_in_proj_kernel, d=d, rc=_pick(tm, 64)),
        out_shape=(jax.ShapeDtypeStruct((r, n), BF16), jax.ShapeDtypeStruct((r, LANES), F32)),
        grid=(r // tm, n // tn),
        in_specs=[
            pl.BlockSpec((tm, d), lambda i, j: (i, 0)),
            pl.BlockSpec((None, 1, 6 * d), lambda i, j: (i * tm // rows_per_mod + mod_base, 0, 0)),
            pl.BlockSpec((1, d), lambda i, j: (0, 0)),
            pl.BlockSpec((None, d, tn), lambda i, j: (layer, 0, j)),
        ],
        out_specs=(
            pl.BlockSpec((tm, tn), lambda i, j: (i, j)),
            pl.BlockSpec((tm, LANES), lambda i, j: (i, 0)),
        ),
        scratch_shapes=[pltpu.VMEM((tm, d), BF16)],
        compiler_params=_cparams("arbitrary", "arbitrary"),
        name="in_proj",
    )(x, mod, g, w)


def _swap16(x):
    lane = lax.broadcasted_iota(jnp.int32, x.shape, 1)
    fwd = pltpu.roll(x, LANES - 16, 1)
    bwd = pltpu.roll(x, 16, 1)
    return jnp.where((lane % 32) < 16, fwd, bwd)


def _rope(x, cos, sin):
    return x * cos + _swap16(x) * sin


def _split3(x):
    hi = x.astype(BF16)
    r1 = x - hi.astype(F32)
    mid = r1.astype(BF16)
    lo = (r1 - mid.astype(F32)).astype(BF16)
    return hi, mid, lo


def _group_mean_sq(x, gm):
    return sum(jnp.dot(part, gm, preferred_element_type=F32) for part in _split3(x * x))


def _da_prep_kernel(q_ref, k_ref, cos_ref, sin_ref, gq_ref, gk_ref, gm_ref, qo_ref, ko_ref, *, qscale):
    cos = cos_ref[...]
    sin = sin_ref[...]
    gm = gm_ref[...]
    lane = lax.broadcasted_iota(jnp.int32, cos.shape, 1)
    first = lane < DA_HALF
    for h in range(DA_HEADS):
        sl = slice(h * LANES, (h + 1) * LANES)
        q = q_ref[:, sl].astype(F32)
        q = q * lax.rsqrt(_group_mean_sq(q, gm) + EPS) * gq_ref[...]
        q = _rope(q, cos, sin) * qscale
        qo_ref[:, 2 * h * LANES:(2 * h + 1) * LANES] = jnp.where(first, q, 0.0).astype(qo_ref.dtype)
        qo_ref[:, (2 * h + 1) * LANES:(2 * h + 2) * LANES] = jnp.where(first, 0.0, q).astype(qo_ref.dtype)
        k = k_ref[:, sl].astype(F32)
        k = k * lax.rsqrt(_group_mean_sq(k, gm) + EPS) * gk_ref[...]
        ko_ref[:, sl] = _rope(k, cos, sin).astype(ko_ref.dtype)


def _da_prep(p, cos, sin, gq, gk, gm):
    r = p.shape[0]
    tm = _pick(min(r, cos.shape[0]), 1024)
    tab_blocks = cos.shape[0] // tm
    w = DA_HEADS * LANES
    qscale = DA_HALF ** -0.5 * LOG2E
    return pl.pallas_call(
        functools.partial(_da_prep_kernel, qscale=qscale),
        out_shape=(jax.ShapeDtypeStruct((r, 2 * w), BF16), jax.ShapeDtypeStruct((r, w), BF16)),
        grid=(r // tm,),
        in_specs=[
            pl.BlockSpec((tm, w), lambda i: (i, C_DAQ // w)),
            pl.BlockSpec((tm, w), lambda i: (i, C_DAK // w)),
            pl.BlockSpec((tm, LANES), lambda i: (i % tab_blocks, 0)),
            pl.BlockSpec((tm, LANES), lambda i: (i % tab_blocks, 0)),
            pl.BlockSpec((1, LANES), lambda i: (0, 0)),
            pl.BlockSpec((1, LANES), lambda i: (0, 0)),
            pl.BlockSpec((LANES, LANES), lambda i: (0, 0)),
        ],
        out_specs=(
            pl.BlockSpec((tm, 2 * w), lambda i: (i, 0)),
            pl.BlockSpec((tm, w), lambda i: (i, 0)),
        ),
        compiler_params=_cparams("arbitrary"),
        name="da_prep",
    )(p, p, cos, sin, gq, gk, gm)


def _mla_prep_kernel(cq_ref, ckv_ref, kpe_ref, cos_ref, sin_ref, gcq_ref, gckv_ref, gq_ref, gk_ref,
                     wq_ref, wkv_ref, qo_ref, ko_ref, vo_ref, *, qscale):
    cos = cos_ref[...]
    sin = sin_ref[...]
    lane = lax.broadcasted_iota(jnp.int32, cos.shape, 1)
    inv_qk = 1.0 / MLA_QK

    cq = cq_ref[...].astype(F32)
    cq = cq * lax.rsqrt(jnp.mean(cq * cq, axis=-1, keepdims=True) + EPS) * gcq_ref[...]
    qf = jnp.dot(cq.astype(BF16), wq_ref[...], preferred_element_type=F32)

    ckv = ckv_ref[...].astype(F32)
    ckv = ckv * lax.rsqrt(jnp.mean(ckv * ckv, axis=-1, keepdims=True) + EPS) * gckv_ref[...]
    kvf = jnp.dot(ckv.astype(BF16), wkv_ref[...], preferred_element_type=F32)

    kpe = jnp.where(lane < MLA_ROPE, kpe_ref[...].astype(F32), 0.0)
    kpe_sq = kpe * kpe
    kpe_rot = _rope(kpe * gk_ref[:, LANES:], cos, sin)

    for h in range(MLA_HEADS):
        q1 = qf[:, h * MLA_PAD:h * MLA_PAD + LANES]
        q2 = qf[:, h * MLA_PAD + LANES:(h + 1) * MLA_PAD]
        ms = jnp.sum(q1 * q1 + q2 * q2, axis=-1, keepdims=True) * inv_qk
        rq = lax.rsqrt(ms + EPS) * qscale
        qo_ref[:, h * MLA_PAD:h * MLA_PAD + LANES] = (q1 * rq * gq_ref[:, :LANES]).astype(qo_ref.dtype)
        q2 = _rope(q2 * rq * gq_ref[:, LANES:], cos, sin)
        qo_ref[:, h * MLA_PAD + LANES:(h + 1) * MLA_PAD] = q2.astype(qo_ref.dtype)

        kn = kvf[:, h * MLA_PAD:h * MLA_PAD + LANES]
        ms = jnp.sum(kn * kn + kpe_sq, axis=-1, keepdims=True) * inv_qk
        rk = lax.rsqrt(ms + EPS)
        ko_ref[:, h * MLA_PAD:h * MLA_PAD + LANES] = (kn * rk * gk_ref[:, :LANES]).astype(ko_ref.dtype)
        ko_ref[:, h * MLA_PAD + LANES:(h + 1) * MLA_PAD] = (kpe_rot * rk).astype(ko_ref.dtype)
        vo_ref[:, h * MLA_V:(h + 1) * MLA_V] = kvf[:, h * MLA_PAD + LANES:(h + 1) * MLA_PAD].astype(vo_ref.dtype)


def _mla_prep(p, cos, sin, gcq, gckv, gq, gk, wq, wkv, layer):
    r = p.shape[0]
    tm = _pick(min(r, cos.shape[0]), 1024)
    tab_blocks = cos.shape[0] // tm
    q_rank = wq.shape[1]
    kv_rank = wkv.shape[1]
    wide = MLA_HEADS * MLA_PAD
    qscale = MLA_QK ** -0.5 * LOG2E
    const = lambda i: (0, 0)
    return pl.pallas_call(
        functools.partial(_mla_prep_kernel, qscale=qscale),
        out_shape=(jax.ShapeDtypeStruct((r, wide), BF16), jax.ShapeDtypeStruct((r, wide), BF16),
                   jax.ShapeDtypeStruct((r, MLA_HEADS * MLA_V), BF16)),
        grid=(r // tm,),
        in_specs=[
            pl.BlockSpec((tm, q_rank), lambda i: (i, C_CQ // q_rank)),
            pl.BlockSpec((tm, kv_rank), lambda i: (i, C_CKV // kv_rank)),
            pl.BlockSpec((tm, LANES), lambda i: (i, C_KPE // LANES)),
            pl.BlockSpec((tm, LANES), lambda i: (i % tab_blocks, 0)),
            pl.BlockSpec((tm, LANES), lambda i: (i % tab_blocks, 0)),
            pl.BlockSpec((1, q_rank), const),
            pl.BlockSpec((1, kv_rank), const),
            pl.BlockSpec((1, MLA_PAD), const),
            pl.BlockSpec((1, MLA_PAD), const),
            pl.BlockSpec((None, q_rank, wide), lambda i: (layer, 0, 0)),
            pl.BlockSpec((None, kv_rank, wide), lambda i: (layer, 0, 0)),
        ],
        out_specs=(
            pl.BlockSpec((tm, wide), lambda i: (i, 0)),
            pl.BlockSpec((tm, wide), lambda i: (i, 0)),
            pl.BlockSpec((tm, MLA_HEADS * MLA_V), lambda i: (i, 0)),
        ),
        compiler_params=_cparams("arbitrary"),
        name="mla_prep",
    )(p, p, p, cos, sin, gcq, gckv, gq, gk, wq, wkv)


def _attn_core(q_ref, k_refs, v_refs, n_comp, dq, kc):
    chunks = []
    for ri, kr in enumerate(k_refs):
        rows = kr.shape[0]
        step = min(kc, rows)
        for r0 in range(0, rows, step):
            chunks.append((ri, r0, step))
    outs = []
    for c in range(n_comp):
        q = q_ref[:, c * dq:(c + 1) * dq]
        m = acc = None
        for ri, r0, step in chunks:
            s = lax.dot_general(q, k_refs[ri][r0:r0 + step, :], (((1,), (1,)), ((), ())),
                                preferred_element_type=F32)
            cm = jnp.max(s, axis=-1, keepdims=True)
            m_new = cm if m is None else jnp.maximum(m, cm)
            p = jnp.exp2(s - m_new)
            vc = v_refs[ri][r0:r0 + step, :]
            pv = jnp.dot(p.astype(BF16), jnp.concatenate([vc, jnp.ones_like(vc)], axis=-1),
                         preferred_element_type=F32)
            acc = pv if m is None else jnp.exp2(m - m_new) * acc + pv
            m = m_new
        dv = acc.shape[1] // 2
        outs.append(acc[:, :dv] / acc[:, dv:])
    return outs


def _da_attn_kernel(*refs, n_kv, lam_init, kc):
    q_ref = refs[0]
    k_refs = refs[1:1 + n_kv]
    v_refs = refs[1 + n_kv:1 + 2 * n_kv]
    lam_ref, g_ref, o_ref = refs[1 + 2 * n_kv:]
    lp = lam_ref[...]
    lam = (jnp.exp(jnp.sum(lp[0:1] * lp[1:2], axis=-1, keepdims=True))
           - jnp.exp(jnp.sum(lp[2:3] * lp[3:4], axis=-1, keepdims=True)) + lam_init)
    o0, o1 = _attn_core(q_ref, k_refs, v_refs, 2, LANES, kc)
    o = o0 - lam * o1
    o = o * lax.rsqrt(jnp.mean(o * o, axis=-1, keepdims=True) + EPS) * g_ref[...]
    o_ref[...] = (o * (1.0 - lam_init)).astype(o_ref.dtype)


def _mla_attn_kernel(*refs, n_kv, kc):
    q_ref = refs[0]
    k_refs = refs[1:1 + n_kv]
    v_refs = refs[1 + n_kv:1 + 2 * n_kv]
    (o_ref,) = refs[1 + 2 * n_kv:]
    (o,) = _attn_core(q_ref, k_refs, v_refs, 1, MLA_PAD, kc)
    o_ref[...] = o.astype(o_ref.dtype)


def _attention(kind, q, kvs, batch, extra, lam_init=0.0):
    heads = DA_HEADS if kind == "da" else MLA_HEADS
    qw = 2 * LANES if kind == "da" else MLA_PAD
    dq = LANES if kind == "da" else MLA_PAD
    dv = DA_V if kind == "da" else MLA_V
    rq = q.shape[0]
    tq = _pick(rq // batch, 1024)
    nq = rq // batch // tq
    kc = 256
    k_specs, v_specs, k_args, v_args = [], [], [], []
    for k, v, vcol in kvs:
        kl = k.shape[0] // batch
        k_specs.append(pl.BlockSpec((kl, dq), lambda b, h, i: (b, h)))
        v_specs.append(pl.BlockSpec((kl, dv), lambda b, h, i, vcol=vcol: (b, vcol // dv + h)))
        k_args.append(k)
        v_args.append(v)
    extra_specs = [pl.BlockSpec(e.shape, lambda b, h, i: (0, 0)) for e in extra]
    if kind == "da":
        body = functools.partial(_da_attn_kernel, n_kv=len(kvs), lam_init=lam_init, kc=kc)
    else:
        body = functools.partial(_mla_attn_kernel, n_kv=len(kvs), kc=kc)
    return pl.pallas_call(
        body,
        out_shape=jax.ShapeDtypeStruct((rq, heads * dv), BF16),
        grid=(batch, heads, nq),
        in_specs=[pl.BlockSpec((tq, qw), lambda b, h, i: (b * nq + i, h))] + k_specs + v_specs + extra_specs,
        out_specs=pl.BlockSpec((tq, dv), lambda b, h, i: (b * nq + i, h)),
        compiler_params=_cparams("arbitrary", "arbitrary", "arbitrary"),
        name=kind + "_attn",
    )(q, *k_args, *v_args, *extra)


def _ml_conv_kernel(xq_ref, xk_ref, wq_ref, wk_ref, bq_ref, bk_ref, q_ref, kt_ref, *, kscale, tc):
    s = xq_ref.shape[0]
    t = lax.broadcasted_iota(jnp.int32, xq_ref.shape, 0)

    def conv_silu(x_ref, w_ref, b_ref):
        x = x_ref[...].astype(F32)
        prev = jnp.where(t == 0, 0.0, pltpu.roll(x, 1, 0))
        nxt = jnp.where(t == s - 1, 0.0, pltpu.roll(x, s - 1, 0))
        y = b_ref[...] + prev * w_ref[0:1, :] + x * w_ref[1:2, :] + nxt * w_ref[2:3, :]
        return y / (1.0 + jnp.exp(-y))

    q_ref[...] = conv_silu(xq_ref, wq_ref, bq_ref).astype(q_ref.dtype)
    k = conv_silu(xk_ref, wk_ref, bk_ref) * kscale
    for r0 in range(0, s, tc):
        kt_ref[:, r0:r0 + tc] = k[r0:r0 + tc, :].T.astype(kt_ref.dtype)


def _ml_conv(p3, conv_w, conv_b):
    nseq, s, _ = p3.shape
    w = ML_HEADS * ML_DK
    q0 = C_MLQ // LANES
    k0 = C_MLK // LANES
    return pl.pallas_call(
        functools.partial(_ml_conv_kernel, kscale=ML_DK ** -0.5, tc=_pick(s, 512)),
        out_shape=(jax.ShapeDtypeStruct((nseq, s, w), BF16), jax.ShapeDtypeStruct((nseq, w, s), BF16)),
        grid=(nseq, ML_HEADS),
        in_specs=[
            pl.BlockSpec((None, s, LANES), lambda b, h: (b, 0, q0 + h)),
            pl.BlockSpec((None, s, LANES), lambda b, h: (b, 0, k0 + h)),
            pl.BlockSpec((3, LANES), lambda b, h: (0, h)),
            pl.BlockSpec((3, LANES), lambda b, h: (0, ML_HEADS + h)),
            pl.BlockSpec((1, LANES), lambda b, h: (0, h)),
            pl.BlockSpec((1, LANES), lambda b, h: (0, ML_HEADS + h)),
        ],
        out_specs=(
            pl.BlockSpec((None, s, LANES), lambda b, h: (b, 0, h)),
            pl.BlockSpec((None, ML_DK, s), lambda b, h: (b, h, 0)),
        ),
        compiler_params=_cparams("arbitrary", "arbitrary"),
        name="ml_conv",
    )(p3, p3, conv_w, conv_w, conv_b, conv_b)


def _ml_gate_kernel(g_ref, b_ref, gt_ref, gg_ref, *, chunk):
    s = g_ref.shape[0]
    row = lax.broadcasted_iota(jnp.int32, (chunk, chunk), 0)
    col = lax.broadcasted_iota(jnp.int32, (chunk, chunk), 1)
    prefix = jnp.where(col <= row, 1.0, 0.0).astype(BF16)
    suffix = jnp.where(col >= row, 1.0, 0.0).astype(BF16)
    lane = lax.broadcasted_iota(jnp.int32, (chunk, LANES), 1)
    is_forget = ((lane % (2 * ML_HEADS)) >= ML_HEADS) & (lane < 4 * ML_HEADS)
    is_fwd = lane < 2 * ML_HEADS
    for r0 in range(0, s, chunk):
        g = g_ref[r0:r0 + chunk, :] + b_ref[...]
        lf = jnp.minimum(g, 0.0) - jnp.log(1.0 + jnp.exp(-jnp.abs(g)))
        parts = _split3(jnp.where(is_forget, lf, 0.0))
        cum_f = sum(jnp.dot(prefix, part, preferred_element_type=F32) for part in parts)
        cum_b = sum(jnp.dot(suffix, part, preferred_element_type=F32) for part in parts)
        out = jnp.where(is_forget, jnp.where(is_fwd, cum_f, cum_b), g)
        gt_ref[r0:r0 + chunk, :] = out
        gg_ref[:, r0:r0 + chunk] = out.T[0:4 * ML_HEADS, :]


def _ml_gates(g3, gate_b):
    nseq, s, _ = g3.shape
    return pl.pallas_call(
        functools.partial(_ml_gate_kernel, chunk=min(ML_CHUNK, s)),
        out_shape=(jax.ShapeDtypeStruct((nseq, s, LANES), F32),
                   jax.ShapeDtypeStruct((nseq, 4 * ML_HEADS, s), F32)),
        grid=(nseq,),
        in_specs=[
            pl.BlockSpec((None, s, LANES), lambda b: (b, 0, 0)),
            pl.BlockSpec((1, LANES), lambda b: (0, 0)),
        ],
        out_specs=(
            pl.BlockSpec((None, s, LANES), lambda b: (b, 0, 0)),
            pl.BlockSpec((None, 4 * ML_HEADS, s), lambda b: (b, 0, 0)),
        ),
        compiler_params=_cparams("arbitrary"),
        name="ml_gates",
    )(g3, gate_b)


def _ml_scan_kernel(*refs, nb, reverse, post):
    if post:
        (q_ref, kt_ref, v_ref, gt_ref, gg_ref, c0_ref, m0_ref, hp_ref, og_ref, gain_ref,
         h_ref, c_out_ref, m_out_ref, c_sc, m_sc) = refs
    else:
        (q_ref, kt_ref, v_ref, gt_ref, gg_ref, c0_ref, m0_ref,
         h_ref, c_out_ref, m_out_ref, c_sc, m_sc) = refs
    step = pl.program_id(0)
    L = q_ref.shape[1]
    nh = ML_HEADS

    @pl.when(step == 0)
    def _():
        c_sc[...] = c0_ref[...]
        m_sc[...] = m0_ref[...]

    row = lax.broadcasted_iota(jnp.int32, (L, L), 0)
    col = lax.broadcasted_iota(jnp.int32, (L, L), 1)
    incl = (col >= row) if reverse else (col <= row)
    last = 0 if reverse else L - 1

    def lanes(h):
        ig_lane = (2 * nh if reverse else 0) + h
        return ig_lane, ig_lane + nh

    col_rep = {}
    for b in range(nb):
        gt = gt_ref[b]
        for h in range(nh):
            ig_lane, lf_lane = lanes(h)
            col_rep[b, h] = (jnp.broadcast_to(gt[:, lf_lane:lf_lane + 1], (L, LANES)),
                             jnp.broadcast_to(gt[:, ig_lane:ig_lane + 1], (L, LANES)))

    for b in range(nb):
        gg = gg_ref[b]
        for h in range(nh):
            bh = b * nh + h
            ig_lane, lf_lane = lanes(h)
            hs = slice(h * ML_DK, (h + 1) * ML_DK)
            q = q_ref[b, :, hs]
            kt = kt_ref[b, hs, :]
            v = v_ref[b, :, hs]
            c_aug = c_sc[bh]
            m_prev = m_sc[bh][0:1, :]

            bcum, ig = col_rep[b, h]
            bcum_row = gg[lf_lane:lf_lane + 1, :]
            ig_row = gg[ig_lane:ig_lane + 1, :]
            total = bcum[last:last + 1, :]

            log_d = jnp.where(incl, bcum[:, :L] - bcum_row + ig_row, -jnp.inf)
            log_prev = bcum + m_prev
            m_t = jnp.maximum(log_prev, jnp.max(log_d, axis=-1, keepdims=True))
            dmat = jnp.exp(log_d - m_t[:, :L])
            w_prev = jnp.exp(log_prev - m_t)
            s = jnp.dot(q, kt, preferred_element_type=F32) * dmat
            lhs = jnp.concatenate([s.astype(BF16), (w_prev * q.astype(F32)).astype(BF16)], axis=-1)
            rhs = jnp.concatenate([jnp.concatenate([v, jnp.ones_like(v)], axis=-1), c_aug.astype(BF16)], axis=0)
            tot = jnp.dot(lhs, rhs, preferred_element_type=F32)
            num = tot[:, :ML_DV]
            den = tot[:, ML_DV:]
            hv = num / jnp.maximum(jnp.abs(den), jnp.exp(-m_t))

            m_new = m_t[last:last + 1, :]
            w_s = jnp.exp(total - bcum + ig - m_new)
            decay = jnp.exp(total + m_prev - m_new)
            wv = jnp.concatenate([w_s * v.astype(F32), w_s], axis=-1).astype(BF16)
            kv = jnp.dot(kt, wv, preferred_element_type=F32)
            c_sc[bh] = jnp.concatenate([decay, decay], axis=-1) * c_aug + kv
            m_sc[bh] = jnp.broadcast_to(m_new, m_sc.shape[1:])

            if post:
                hv = hv + hp_ref[b, :, hs]
                hv = hv * lax.rsqrt(jnp.mean(hv * hv, axis=-1, keepdims=True) + EPS) * gain_ref[:, hs]
                og = og_ref[b, :, hs].astype(F32)
                hv = hv / (1.0 + jnp.exp(-og))
            h_ref[b, :, hs] = hv.astype(h_ref.dtype)

    @pl.when(step == pl.num_programs(0) - 1)
    def _():
        c_out_ref[...] = c_sc[...]
        m_out_ref[...] = m_sc[...]


def _ml_scan(q3, kt3, p3, gt3, gg3, state, reverse, post_args=None):
    nb, s, _ = q3.shape
    assert ML_DK == LANES and ML_DV == LANES and min(ML_CHUNK, s) <= LANES
    L = min(ML_CHUNK, s)
    nc = s // L
    w = ML_HEADS * ML_DK
    c0, m0 = state
    post = post_args is not None
    cidx = (lambda c: nc - 1 - c) if reverse else (lambda c: c)
    in_specs = [
        pl.BlockSpec((nb, L, w), lambda c: (0, cidx(c), 0)),
        pl.BlockSpec((nb, w, L), lambda c: (0, 0, cidx(c))),
        pl.BlockSpec((nb, L, w), lambda c: (0, cidx(c), C_MLV // w)),
        pl.BlockSpec((nb, L, LANES), lambda c: (0, cidx(c), 0)),
        pl.BlockSpec((nb, 4 * ML_HEADS, L), lambda c: (0, 0, cidx(c))),
        pl.BlockSpec(c0.shape, lambda c: (0, 0, 0)),
        pl.BlockSpec(m0.shape, lambda c: (0, 0, 0)),
    ]
    args = [q3, kt3, p3, gt3, gg3, c0, m0]
    if post:
        h_prev, gain = post_args
        in_specs += [
            pl.BlockSpec((nb, L, w), lambda c: (0, cidx(c), 0)),
            pl.BlockSpec((nb, L, w), lambda c: (0, cidx(c), C_MLO // w)),
            pl.BlockSpec((1, w), lambda c: (0, 0)),
        ]
        args += [h_prev, p3, gain]
    return pl.pallas_call(
        functools.partial(_ml_scan_kernel, nb=nb, reverse=reverse, post=post),
        out_shape=(jax.ShapeDtypeStruct((nb, s, w), BF16 if post else F32),
                   jax.ShapeDtypeStruct(c0.shape, F32), jax.ShapeDtypeStruct(m0.shape, F32)),
        grid=(nc,),
        in_specs=in_specs,
        out_specs=(
            pl.BlockSpec((nb, L, w), lambda c: (0, cidx(c), 0)),
            pl.BlockSpec(c0.shape, lambda c: (0, 0, 0)),
            pl.BlockSpec(m0.shape, lambda c: (0, 0, 0)),
        ),
        scratch_shapes=[pltpu.VMEM(c0.shape, F32), pltpu.VMEM(m0.shape, F32)],
        compiler_params=_cparams("arbitrary"),
        name="ml_scan",
    )(*args)


def _out_proj_kernel(da_ref, ml_ref, mla_ref, w_ref, x_ref, gate_ref, h_ref):
    n_da = da_ref.shape[1]
    n_ml = ml_ref.shape[1]
    acc = jnp.dot(da_ref[...], w_ref[0:n_da, :], preferred_element_type=F32)
    acc += jnp.dot(ml_ref[...], w_ref[n_da:n_da + n_ml, :], preferred_element_type=F32)
    acc += jnp.dot(mla_ref[...], w_ref[n_da + n_ml:, :], preferred_element_type=F32)
    h_ref[...] = x_ref[...] + gate_ref[...] * acc


def _out_proj(da, ml, mla, w, layer, x, mod, rows_per_mod, mod_base):
    r, d = x.shape
    tm = _pick(min(r, rows_per_mod), 1024)
    tn = _pick(d, 1024)
    nt = d // tn
    return pl.pallas_call(
        _out_proj_kernel,
        out_shape=jax.ShapeDtypeStruct((r, d), F32),
        grid=(nt, r // tm),
        in_specs=[
            pl.BlockSpec((tm, da.shape[1]), lambda j, i: (i, 0)),
            pl.BlockSpec((tm, ml.shape[1]), lambda j, i: (i, 0)),
            pl.BlockSpec((tm, mla.shape[1]), lambda j, i: (i, 0)),
            pl.BlockSpec((None, w.shape[1], tn), lambda j, i: (layer, 0, j)),
            pl.BlockSpec((tm, tn), lambda j, i: (i, j)),
            pl.BlockSpec((None, 1, tn), lambda j, i: (i * tm // rows_per_mod + mod_base, 0, 2 * nt + j)),
        ],
        out_specs=pl.BlockSpec((tm, tn), lambda j, i: (i, j)),
        compiler_params=_cparams("arbitrary", "arbitrary"),
        name="out_proj",
    )(da, ml, mla, w, x, mod)


def _ffn_kernel(h_ref, mod_ref, g_ref, wg_ref, wu_ref, wd_ref, o_ref, hn_ref, *, d, rc):
    j = pl.program_id(1)

    @pl.when(j == 0)
    def _():
        shift = mod_ref[:, 3 * d:4 * d]
        scale1 = 1.0 + mod_ref[:, 4 * d:5 * d]
        _norm_rows(h_ref, g_ref, shift, scale1, hn_ref, rc)
        o_ref[...] = h_ref[...]

    hn = hn_ref[...]
    gate = jnp.dot(hn, wg_ref[...], preferred_element_type=F32)
    up = jnp.dot(hn, wu_ref[...], preferred_element_type=F32)
    act = (gate / (1.0 + jnp.exp(-gate)) * up).astype(BF16)
    o_ref[...] += mod_ref[:, 5 * d:6 * d] * jnp.dot(act, wd_ref[...], preferred_element_type=F32)


def _ffn(h, mod, g, w_gu, w_down, layer, rows_per_mod, mod_base):
    r, d = h.shape
    ff = w_down.shape[1]
    tm = _pick(min(r, rows_per_mod), 1024)
    th = 512 if ff % 512 == 0 else 256
    nh = ff // th
    return pl.pallas_call(
        functools.partial(_ffn_kernel, d=d, rc=_pick(tm, 64)),
        out_shape=jax.ShapeDtypeStruct((r, d), F32),
        grid=(r // tm, nh),
        in_specs=[
            pl.BlockSpec((tm, d), lambda i, j: (i, 0)),
            pl.BlockSpec((None, 1, 6 * d), lambda i, j: (i * tm // rows_per_mod + mod_base, 0, 0)),
            pl.BlockSpec((1, d), lambda i, j: (0, 0)),
            pl.BlockSpec((None, d, th), lambda i, j: (layer, 0, j)),
            pl.BlockSpec((None, d, th), lambda i, j: (layer, 0, nh + j)),
            pl.BlockSpec((None, th, d), lambda i, j: (layer, j, 0)),
        ],
        out_specs=pl.BlockSpec((tm, d), lambda i, j: (i, 0)),
        scratch_shapes=[pltpu.VMEM((tm, d), BF16)],
        compiler_params=_cparams("arbitrary", "arbitrary"),
        name="ffn",
    )(h, mod, g, w_gu, w_gu, w_down)


def _rope_tables(n_lat):
    pos = jnp.arange(n_lat, dtype=jnp.int32)
    r = (pos // GRID_W).astype(F32)
    c = (pos % GRID_W).astype(F32)
    half = ROPE_DIM // 4
    inv = ROPE_BASE ** (-jnp.arange(half, dtype=F32) / half)
    ar = r[:, None] * inv
    ac = c[:, None] * inv
    cos64 = jnp.concatenate([jnp.cos(ar), jnp.cos(ar), jnp.cos(ac), jnp.cos(ac)], axis=-1)
    sin64 = jnp.concatenate([-jnp.sin(ar), jnp.sin(ar), -jnp.sin(ac), jnp.sin(ac)], axis=-1)
    return cos64, sin64


def _pad_cols(a, width):
    return jnp.pad(a, ((0, 0),) * (a.ndim - 1) + ((0, width - a.shape[-1]),))


def _layout_w_in(w):
    gates0 = C_CQ
    gates1 = gates0 + 4 * ML_HEADS
    main = w[..., :gates0]
    gates = w[..., gates0:gates1]
    rest = w[..., gates1:]
    n_cq_ckv = C_KPE - C_CQ
    cq_ckv = rest[..., :n_cq_ckv]
    kpe = rest[..., n_cq_ckv:]
    return jnp.concatenate([main, cq_ckv, _pad_cols(kpe, LANES), _pad_cols(gates, LANES)], axis=-1)


def kernel(x, c, ctx, c_ctx, mod_w, mod_b, norm1_g, norm2_g, w_in, da_qk_g, da_lambda, da_out_g,
           ml_conv_w, ml_conv_b, ml_gate_b, ml_out_g, mla_q_norm_g, mla_kv_norm_g, mla_w_uq, mla_w_ukv,
           mla_qk_g, w_out, ffn_w_gu, ffn_w_down):
    batch, seq, d = x.shape
    ctx_len = ctx.shape[1]
    depth = mod_w.shape[0]
    in_dtype = x.dtype
    assert w_in.shape[2] == C_CQ + 4 * ML_HEADS + (C_KPE - C_CQ) + MLA_ROPE

    n_mod_rows = -(-(batch + 1) // 8) * 8
    cc = jnp.concatenate([c, c_ctx[None, :], jnp.zeros((n_mod_rows - batch - 1, d), F32)], axis=0)
    mod_all = _modulation(cc, mod_w, mod_b)

    cos64, sin64 = _rope_tables(seq)
    cos_da = jnp.tile(cos64, (1, 2))
    sin_da = jnp.tile(sin64, (1, 2))
    cos_mla = jnp.concatenate([cos64, jnp.ones_like(cos64)], axis=-1)
    sin_mla = jnp.concatenate([sin64, jnp.zeros_like(sin64)], axis=-1)
    tab_rows = _pick(batch * ctx_len, 1024)
    cos_id = jnp.ones((tab_rows, LANES), F32)
    sin_id = jnp.zeros((tab_rows, LANES), F32)
    lane = jnp.arange(LANES)
    group_mean = jnp.where((lane[:, None] // DA_HALF) == (lane[None, :] // DA_HALF),
                           1.0 / DA_HALF, 0.0).astype(BF16)

    w_in_b = _layout_w_in(w_in).astype(BF16)
    w_out_b = w_out.astype(BF16)
    w_gu_b = ffn_w_gu.astype(BF16)
    w_down_b = ffn_w_down.astype(BF16)
    wq_b = jnp.pad(mla_w_uq.reshape(depth, -1, MLA_HEADS, MLA_QK),
                   ((0, 0), (0, 0), (0, 0), (0, MLA_PAD - MLA_QK))).reshape(depth, -1, MLA_HEADS * MLA_PAD).astype(BF16)
    wkv_b = mla_w_ukv.astype(BF16)

    xl = x.reshape(batch * seq, d)
    xc = ctx.reshape(batch * ctx_len, d)
    huge = batch * max(seq, ctx_len) * 2

    for l in range(depth):
        need_ctx = l < depth - 1
        lam_init = 0.8 - 0.6 * math.exp(-0.3 * l)
        mod = mod_all[l].reshape(n_mod_rows, 1, 6 * d)
        g1 = norm1_g[l].reshape(1, d)
        g2 = norm2_g[l].reshape(1, d)
        gq_da = jnp.tile(da_qk_g[l, 0], 2).reshape(1, LANES)
        gk_da = jnp.tile(da_qk_g[l, 1], 2).reshape(1, LANES)
        gq_mla = _pad_cols(mla_qk_g[l, 0].reshape(1, MLA_QK), MLA_PAD)
        gk_mla = _pad_cols(mla_qk_g[l, 1].reshape(1, MLA_QK), MLA_PAD)
        gcq = mla_q_norm_g[l].reshape(1, -1)
        gckv = mla_kv_norm_g[l].reshape(1, -1)
        gate_b = _pad_cols(ml_gate_b[l].reshape(1, -1), LANES)
        conv_w = ml_conv_w[l]
        conv_b = ml_conv_b[l].reshape(1, -1)
        ml_gain = ml_out_g[l].reshape(1, -1)
        da_gain = da_out_g[l].reshape(1, -1)
        lam_p = da_lambda[l]

        p_l, g_l = _in_proj(xl, mod, g1, w_in_b, l, seq, 0)
        p_c, g_c = _in_proj(xc, mod, g1, w_in_b, l, huge, batch)

        qd_l, kd_l = _da_prep(p_l, cos_da, sin_da, gq_da, gk_da, group_mean)
        qd_c, kd_c = _da_prep(p_c, cos_id, sin_id, gq_da, gk_da, group_mean)
        da_l = _attention("da", qd_l, [(kd_l, p_l, C_DAV), (kd_c, p_c, C_DAV)], batch,
                          [lam_p, da_gain], lam_init)
        qm_l, km_l, vm_l = _mla_prep(p_l, cos_mla, sin_mla, gcq, gckv, gq_mla, gk_mla, wq_b, wkv_b, l)
        qm_c, km_c, vm_c = _mla_prep(p_c, cos_id, sin_id, gcq, gckv, gq_mla, gk_mla, wq_b, wkv_b, l)
        mla_l = _attention("mla", qm_l, [(km_l, vm_l, 0), (km_c, vm_c, 0)], batch, [])

        p3_l = p_l.reshape(batch, seq, P_COLS)
        p3_c = p_c.reshape(batch, ctx_len, P_COLS)
        q_l, kt_l = _ml_conv(p3_l, conv_w, conv_b)
        q_c, kt_c = _ml_conv(p3_c, conv_w, conv_b)
        gt_l, gg_l = _ml_gates(g_l.reshape(batch, seq, LANES), gate_b)
        gt_c, gg_c = _ml_gates(g_c.reshape(batch, ctx_len, LANES), gate_b)
        s0 = (jnp.zeros((batch * ML_HEADS, ML_DK, 2 * ML_DV), F32),
              jnp.zeros((batch * ML_HEADS, 8, LANES), F32))
        h_cf, c_f, m_f = _ml_scan(q_c, kt_c, p3_c, gt_c, gg_c, s0, False)
        h_lf, _, _ = _ml_scan(q_l, kt_l, p3_l, gt_l, gg_l, (c_f, m_f), False)
        if need_ctx:
            ml_c, c_b, m_b = _ml_scan(q_c, kt_c, p3_c, gt_c, gg_c, s0, True, (h_cf, ml_gain))
        else:
            _, c_b, m_b = _ml_scan(q_c, kt_c, p3_c, gt_c, gg_c, s0, True)
        ml_l, _, _ = _ml_scan(q_l, kt_l, p3_l, gt_l, gg_l, (c_b, m_b), True, (h_lf, ml_gain))

        h_l = _out_proj(da_l, ml_l.reshape(batch * seq, -1), mla_l, w_out_b, l, xl, mod, seq, 0)
        xl = _ffn(h_l, mod, g2, w_gu_b, w_down_b, l, seq, 0)
        if need_ctx:
            da_c = _attention("da", qd_c, [(kd_c, p_c, C_DAV)], batch, [lam_p, da_gain], lam_init)
            mla_c = _attention("mla", qm_c, [(km_c, vm_c, 0)], batch, [])
            h_c = _out_proj(da_c, ml_c.reshape(batch * ctx_len, -1), mla_c, w_out_b, l, xc, mod, huge, batch)
            xc = _ffn(h_c, mod, g2, w_gu_b, w_down_b, l, huge, batch)

    return xl.reshape(batch, seq, d).astype(in_dtype)
```

```python
import functools
import math

import jax
import jax.numpy as jnp
from jax import lax
from jax.experimental import pallas as pl
from jax.experimental.pallas import tpu as pltpu

F32 = jnp.float32
BF16 = jnp.bfloat16

DA_HEADS = 4
DA_HALF = 64
DA_V = 2 * DA_HALF
ML_HEADS = 4
ML_DK = 128
ML_DV = 128
MLA_HEADS = 8
MLA_NOPE = 128
MLA_ROPE = 64
MLA_V = 128
MLA_QK = MLA_NOPE + MLA_ROPE
MLA_PAD = 256
GRID_W = 64
ROPE_DIM = 64
ROPE_BASE = 10000.0
EPS = 1e-6
LOG2E = 1.4426950408889634

LANES = 128
ML_CHUNK = 128
VMEM_LIMIT = 56 * 1024 * 1024

C_DAQ, C_DAK, C_DAV = 0, 512, 1024
C_MLQ, C_MLK, C_MLV, C_MLO = 1536, 2048, 2560, 3072
C_CQ, C_CKV, C_KPE, C_GATE = 3584, 4096, 4352, 4480
P_COLS = 4608


def _cparams(*sem):
    return pltpu.CompilerParams(dimension_semantics=sem, vmem_limit_bytes=VMEM_LIMIT)


def _pick(n, pref):
    if n <= pref:
        return n
    t = pref
    while n % t:
        t //= 2
    return t


def _mod_kernel(c_ref, w_ref, b_ref, o_ref):
    c = c_ref[...]
    s = (c / (1.0 + jnp.exp(-c))).astype(BF16)
    o_ref[...] = jnp.dot(s, w_ref[...].astype(BF16), preferred_element_type=F32) + b_ref[...]


def _modulation(cc, mod_w, mod_b):
    depth, d, n = mod_w.shape
    tn = _pick(n, 1024)
    return pl.pallas_call(
        _mod_kernel,
        out_shape=jax.ShapeDtypeStruct((depth, cc.shape[0], n), F32),
        grid=(depth, n // tn),
        in_specs=[
            pl.BlockSpec((cc.shape[0], d), lambda l, j: (0, 0)),
            pl.BlockSpec((None, d, tn), lambda l, j: (l, 0, j)),
            pl.BlockSpec((None, 1, tn), lambda l, j: (l, 0, j)),
        ],
        out_specs=pl.BlockSpec((None, cc.shape[0], tn), lambda l, j: (l, 0, j)),
        compiler_params=_cparams("arbitrary", "arbitrary"),
        name="modulation",
    )(cc, mod_w, mod_b.reshape(depth, 1, n))


def _norm_rows(x_ref, g_ref, shift, scale1, out_ref, rc):
    tm = x_ref.shape[0]
    gs = g_ref[...] * scale1

    def body(r, carry):
        rows = pl.ds(pl.multiple_of(r * rc, rc), rc)
        x = x_ref[rows, :]
        ms = jnp.mean(x * x, axis=-1, keepdims=True)
        out_ref[rows, :] = (x * lax.rsqrt(ms + EPS) * gs + shift).astype(out_ref.dtype)
        return carry

    lax.fori_loop(0, tm // rc, body, 0, unroll=2)


def _in_proj_kernel(x_ref, mod_ref, g_ref, w_ref, p_ref, gate_ref, xn_ref, *, d, rc):
    n = pl.program_id(1)

    @pl.when(n == 0)
    def _():
        shift = mod_ref[:, 0:d]
        scale1 = 1.0 + mod_ref[:, d:2 * d]
        _norm_rows(x_ref, g_ref, shift, scale1, xn_ref, rc)

    acc = jnp.dot(xn_ref[...], w_ref[...], preferred_element_type=F32)
    p_ref[...] = acc.astype(p_ref.dtype)

    @pl.when(n == pl.num_programs(1) - 1)
    def _():
        gate_ref[...] = acc[:, acc.shape[1] - LANES:]


def _in_proj(x, mod, g, w, layer, rows_per_mod, mod_base):
    r, d = x.shape
    n = w.shape[2]
    tm = _pick(min(r, rows_per_mod), 1024)
    tn = 1536
    assert n % tn == 0
    return pl.pallas_call(
        functools.partial(_in_proj_kernel, d=d, rc=_pick(tm, 64)),
        out_shape=(jax.ShapeDtypeStruct((r, n), BF16), jax.ShapeDtypeStruct((r, LANES), F32)),
        grid=(r // tm, n // tn),
        in_specs=[
            pl.BlockSpec((tm, d), lambda i, j: (i, 0)),
            pl.BlockSpec((None, 1, 6 * d), lambda i, j: (i * tm // rows_per_mod + mod_base, 0, 0)),
            pl.BlockSpec((1, d), lambda i, j: (0, 0)),
            pl.BlockSpec((None, d, tn), lambda i, j: (layer, 0, j)),
        ],
        out_specs=(
            pl.BlockSpec((tm, tn), lambda i, j: (i, j)),
            pl.BlockSpec((tm, LANES), lambda i, j: (i, 0)),
        ),
        scratch_shapes=[pltpu.VMEM((tm, d), BF16)],
        compiler_params=_cparams("arbitrary", "arbitrary"),
        name="in_proj",
    )(x, mod, g, w)


def _swap16(x):
    lane = lax.broadcasted_iota(jnp.int32, x.shape, 1)
    fwd = pltpu.roll(x, LANES - 16, 1)
    bwd = pltpu.roll(x, 16, 1)
    return jnp.where((lane % 32) < 16, fwd, bwd)


def _rope(x, cos, sin):
    return x * cos + _swap16(x) * sin


def _split3(x):
    hi = x.astype(BF16)
    r1 = x - hi.astype(F32)
    mid = r1.astype(BF16)
    lo = (r1 - mid.astype(F32)).astype(BF16)
    return hi, mid, lo


def _group_mean_sq(x, gm):
    return sum(jnp.dot(part, gm, preferred_element_type=F32) for part in _split3(x * x))


def _da_prep_kernel(q_ref, k_ref, cos_ref, sin_ref, gq_ref, gk_ref, gm_ref, qo_ref, ko_ref, *, qscale):
    cos = cos_ref[...]
    sin = sin_ref[...]
    gm = gm_ref[...]
    lane = lax.broadcasted_iota(jnp.int32, cos.shape, 1)
    first = lane < DA_HALF
    for h in range(DA_HEADS):
        sl = slice(h * LANES, (h + 1) * LANES)
        q = q_ref[:, sl].astype(F32)
        q = q * lax.rsqrt(_group_mean_sq(q, gm) + EPS) * gq_ref[...]
        q = _rope(q, cos, sin) * qscale
        qo_ref[:, 2 * h * LANES:(2 * h + 1) * LANES] = jnp.where(first, q, 0.0).astype(qo_ref.dtype)
        qo_ref[:, (2 * h + 1) * LANES:(2 * h + 2) * LANES] = jnp.where(first, 0.0, q).astype(qo_ref.dtype)
        k = k_ref[:, sl].astype(F32)
        k = k * lax.rsqrt(_group_mean_sq(k, gm) + EPS) * gk_ref[...]
        ko_ref[:, sl] = _rope(k, cos, sin).astype(ko_ref.dtype)


def _da_prep(p, cos, sin, gq, gk, gm):
    r = p.shape[0]
    tm = _pick(min(r, cos.shape[0]), 1024)
    tab_blocks = cos.shape[0] // tm
    w = DA_HEADS * LANES
    qscale = DA_HALF ** -0.5 * LOG2E
    return pl.pallas_call(
        functools.partial(_da_prep_kernel, qscale=qscale),
        out_shape=(jax.ShapeDtypeStruct((r, 2 * w), BF16), jax.ShapeDtypeStruct((r, w), BF16)),
        grid=(r // tm,),
        in_specs=[
            pl.BlockSpec((tm, w), lambda i: (i, C_DAQ // w)),
            pl.BlockSpec((tm, w), lambda i: (i, C_DAK // w)),
            pl.BlockSpec((tm, LANES), lambda i: (i % tab_blocks, 0)),
            pl.BlockSpec((tm, LANES), lambda i: (i % tab_blocks, 0)),
            pl.BlockSpec((1, LANES), lambda i: (0, 0)),
            pl.BlockSpec((1, LANES), lambda i: (0, 0)),
            pl.BlockSpec((LANES, LANES), lambda i: (0, 0)),
        ],
        out_specs=(
            pl.BlockSpec((tm, 2 * w), lambda i: (i, 0)),
            pl.BlockSpec((tm, w), lambda i: (i, 0)),
        ),
        compiler_params=_cparams("arbitrary"),
        name="da_prep",
    )(p, p, cos, sin, gq, gk, gm)


def _mla_prep_kernel(cq_ref, ckv_ref, kpe_ref, cos_ref, sin_ref, gcq_ref, gckv_ref, gq_ref, gk_ref,
                     wq_ref, wkv_ref, qo_ref, ko_ref, vo_ref, *, qscale):
    cos = cos_ref[...]
    sin = sin_ref[...]
    lane = lax.broadcasted_iota(jnp.int32, cos.shape, 1)
    inv_qk = 1.0 / MLA_QK

    cq = cq_ref[...].astype(F32)
    cq = cq * lax.rsqrt(jnp.mean(cq * cq, axis=-1, keepdims=True) + EPS) * gcq_ref[...]
    qf = jnp.dot(cq.astype(BF16), wq_ref[...], preferred_element_type=F32)

    ckv = ckv_ref[...].astype(F32)
    ckv = ckv * lax.rsqrt(jnp.mean(ckv * ckv, axis=-1, keepdims=True) + EPS) * gckv_ref[...]
    kvf = jnp.dot(ckv.astype(BF16), wkv_ref[...], preferred_element_type=F32)

    kpe = jnp.where(lane < MLA_ROPE, kpe_ref[...].astype(F32), 0.0)
    kpe_sq = kpe * kpe
    kpe_rot = _rope(kpe * gk_ref[:, LANES:], cos, sin)

    for h in range(MLA_HEADS):
        q1 = qf[:, h * MLA_PAD:h * MLA_PAD + LANES]
        q2 = qf[:, h * MLA_PAD + LANES:(h + 1) * MLA_PAD]
        ms = jnp.sum(q1 * q1 + q2 * q2, axis=-1, keepdims=True) * inv_qk
        rq = lax.rsqrt(ms + EPS) * qscale
        qo_ref[:, h * MLA_PAD:h * MLA_PAD + LANES] = (q1 * rq * gq_ref[:, :LANES]).astype(qo_ref.dtype)
        q2 = _rope(q2 * rq * gq_ref[:, LANES:], cos, sin)
        qo_ref[:, h * MLA_PAD + LANES:(h + 1) * MLA_PAD] = q2.astype(qo_ref.dtype)

        kn = kvf[:, h * MLA_PAD:h * MLA_PAD + LANES]
        ms = jnp.sum(kn * kn + kpe_sq, axis=-1, keepdims=True) * inv_qk
        rk = lax.rsqrt(ms + EPS)
        ko_ref[:, h * MLA_PAD:h * MLA_PAD + LANES] = (kn * rk * gk_ref[:, :LANES]).astype(ko_ref.dtype)
        ko_ref[:, h * MLA_PAD + LANES:(h + 1) * MLA_PAD] = (kpe_rot * rk).astype(ko_ref.dtype)
        vo_ref[:, h * MLA_V:(h + 1) * MLA_V] = kvf[:, h * MLA_PAD + LANES:(h + 1) * MLA_PAD].astype(vo_ref.dtype)


def _mla_prep(p, cos, sin, gcq, gckv, gq, gk, wq, wkv, layer):
    r = p.shape[0]
    tm = _pick(min(r, cos.shape[0]), 1024)
    tab_blocks = cos.shape[0] // tm
    q_rank = wq.shape[1]
    kv_rank = wkv.shape[1]
    wide = MLA_HEADS * MLA_PAD
    qscale = MLA_QK ** -0.5 * LOG2E
    const = lambda i: (0, 0)
    return pl.pallas_call(
        functools.partial(_mla_prep_kernel, qscale=qscale),
        out_shape=(jax.ShapeDtypeStruct((r, wide), BF16), jax.ShapeDtypeStruct((r, wide), BF16),
                   jax.ShapeDtypeStruct((r, MLA_HEADS * MLA_V), BF16)),
        grid=(r // tm,),
        in_specs=[
            pl.BlockSpec((tm, q_rank), lambda i: (i, C_CQ // q_rank)),
            pl.BlockSpec((tm, kv_rank), lambda i: (i, C_CKV // kv_rank)),
            pl.BlockSpec((tm, LANES), lambda i: (i, C_KPE // LANES)),
            pl.BlockSpec((tm, LANES), lambda i: (i % tab_blocks, 0)),
            pl.BlockSpec((tm, LANES), lambda i: (i % tab_blocks, 0)),
            pl.BlockSpec((1, q_rank), const),
            pl.BlockSpec((1, kv_rank), const),
            pl.BlockSpec((1, MLA_PAD), const),
            pl.BlockSpec((1, MLA_PAD), const),
            pl.BlockSpec((None, q_rank, wide), lambda i: (layer, 0, 0)),
            pl.BlockSpec((None, kv_rank, wide), lambda i: (layer, 0, 0)),
        ],
        out_specs=(
            pl.BlockSpec((tm, wide), lambda i: (i, 0)),
            pl.BlockSpec((tm, wide), lambda i: (i, 0)),
            pl.BlockSpec((tm, MLA_HEADS * MLA_V), lambda i: (i, 0)),
        ),
        compiler_params=_cparams("arbitrary"),
        name="mla_prep",
    )(p, p, p, cos, sin, gcq, gckv, gq, gk, wq, wkv)


def _attn_core(q_ref, k_refs, v_refs, n_comp, dq, kc):
    chunks = []
    for ri, kr in enumerate(k_refs):
        rows = kr.shape[0]
        step = min(kc, rows)
        for r0 in range(0, rows, step):
            chunks.append((ri, r0, step))
    outs = []
    for c in range(n_comp):
        q = q_ref[:, c * dq:(c + 1) * dq]
        m = acc = None
        for ri, r0, step in chunks:
            s = lax.dot_general(q, k_refs[ri][r0:r0 + step, :], (((1,), (1,)), ((), ())),
                                preferred_element_type=F32)
            cm = jnp.max(s, axis=-1, keepdims=True)
            m_new = cm if m is None else jnp.maximum(m, cm)
            p = jnp.exp2(s - m_new)
            vc = v_refs[ri][r0:r0 + step, :]
            pv = jnp.dot(p.astype(BF16), jnp.concatenate([vc, jnp.ones_like(vc)], axis=-1),
                         preferred_element_type=F32)
            acc = pv if m is None else jnp.exp2(m - m_new) * acc + pv
            m = m_new
        dv = acc.shape[1] // 2
        outs.append(acc[:, :dv] / acc[:, dv:])
    return outs


def _da_attn_kernel(*refs, n_kv, lam_init, kc):
    q_ref = refs[0]
    k_refs = refs[1:1 + n_kv]
    v_refs = refs[1 + n_kv:1 + 2 * n_kv]
    lam_ref, g_ref, o_ref = refs[1 + 2 * n_kv:]
    lp = lam_ref[...]
    lam = (jnp.exp(jnp.sum(lp[0:1] * lp[1:2], axis=-1, keepdims=True))
           - jnp.exp(jnp.sum(lp[2:3] * lp[3:4], axis=-1, keepdims=True)) + lam_init)
    o0, o1 = _attn_core(q_ref, k_refs, v_refs, 2, LANES, kc)
    o = o0 - lam * o1
    o = o * lax.rsqrt(jnp.mean(o * o, axis=-1, keepdims=True) + EPS) * g_ref[...]
    o_ref[...] = (o * (1.0 - lam_init)).astype(o_ref.dtype)


def _mla_attn_kernel(*refs, n_kv, kc):
    q_ref = refs[0]
    k_refs = refs[1:1 + n_kv]
    v_refs = refs[1 + n_kv:1 + 2 * n_kv]
    (o_ref,) = refs[1 + 2 * n_kv:]
    (o,) = _attn_core(q_ref, k_refs, v_refs, 1, MLA_PAD, kc)
    o_ref[...] = o.astype(o_ref.dtype)


def _attention(kind, q, kvs, batch, extra, lam_init=0.0):
    heads = DA_HEADS if kind == "da" else MLA_HEADS
    qw = 2 * LANES if kind == "da" else MLA_PAD
    dq = LANES if kind == "da" else MLA_PAD
    dv = DA_V if kind == "da" else MLA_V
    rq = q.shape[0]
    tq = _pick(rq // batch, 2048)
    nq = rq // batch // tq
    kc = 256
    k_specs, v_specs, k_args, v_args = [], [], [], []
    for k, v, vcol in kvs:
        kl = k.shape[0] // batch
        k_specs.append(pl.BlockSpec((kl, dq), lambda b, h, i: (b, h)))
        v_specs.append(pl.BlockSpec((kl, dv), lambda b, h, i, vcol=vcol: (b, vcol // dv + h)))
        k_args.append(k)
        v_args.append(v)
    extra_specs = [pl.BlockSpec(e.shape, lambda b, h, i: (0, 0)) for e in extra]
    if kind == "da":
        body = functools.partial(_da_attn_kernel, n_kv=len(kvs), lam_init=lam_init, kc=kc)
    else:
        body = functools.partial(_mla_attn_kernel, n_kv=len(kvs), kc=kc)
    return pl.pallas_call(
        body,
        out_shape=jax.ShapeDtypeStruct((rq, heads * dv), BF16),
        grid=(batch, heads, nq),
        in_specs=[pl.BlockSpec((tq, qw), lambda b, h, i: (b * nq + i, h))] + k_specs + v_specs + extra_specs,
        out_specs=pl.BlockSpec((tq, dv), lambda b, h, i: (b * nq + i, h)),
        compiler_params=_cparams("arbitrary", "arbitrary", "arbitrary"),
        name=kind + "_attn",
    )(q, *k_args, *v_args, *extra)


def _ml_conv_kernel(xq_ref, xk_ref, wq_ref, wk_ref, bq_ref, bk_ref, q_ref, kt_ref, *, kscale, tc):
    s = xq_ref.shape[0]
    t = lax.broadcasted_iota(jnp.int32, xq_ref.shape, 0)

    def conv_silu(x_ref, w_ref, b_ref):
        x = x_ref[...].astype(F32)
        prev = jnp.where(t == 0, 0.0, pltpu.roll(x, 1, 0))
        nxt = jnp.where(t == s - 1, 0.0, pltpu.roll(x, s - 1, 0))
        y = b_ref[...] + prev * w_ref[0:1, :] + x * w_ref[1:2, :] + nxt * w_ref[2:3, :]
        return y / (1.0 + jnp.exp(-y))

    q_ref[...] = conv_silu(xq_ref, wq_ref, bq_ref).astype(q_ref.dtype)
    k = conv_silu(xk_ref, wk_ref, bk_ref) * kscale
    for r0 in range(0, s, tc):
        kt_ref[:, r0:r0 + tc] = k[r0:r0 + tc, :].T.astype(kt_ref.dtype)


def _ml_conv(p3, conv_w, conv_b):
    nseq, s, _ = p3.shape
    w = ML_HEADS * ML_DK
    q0 = C_MLQ // LANES
    k0 = C_MLK // LANES
    return pl.pallas_call(
        functools.partial(_ml_conv_kernel, kscale=ML_DK ** -0.5, tc=_pick(s, 512)),
        out_shape=(jax.ShapeDtypeStruct((nseq, s, w), BF16), jax.ShapeDtypeStruct((nseq, w, s), BF16)),
        grid=(nseq, ML_HEADS),
        in_specs=[
            pl.BlockSpec((None, s, LANES), lambda b, h: (b, 0, q0 + h)),
            pl.BlockSpec((None, s, LANES), lambda b, h: (b, 0, k0 + h)),
            pl.BlockSpec((3, LANES), lambda b, h: (0, h)),
            pl.BlockSpec((3, LANES), lambda b, h: (0, ML_HEADS + h)),
            pl.BlockSpec((1, LANES), lambda b, h: (0, h)),
            pl.BlockSpec((1, LANES), lambda b, h: (0, ML_HEADS + h)),
        ],
        out_specs=(
            pl.BlockSpec((None, s, LANES), lambda b, h: (b, 0, h)),
            pl.BlockSpec((None, ML_DK, s), lambda b, h: (b, h, 0)),
        ),
        compiler_params=_cparams("arbitrary", "arbitrary"),
        name="ml_conv",
    )(p3, p3, conv_w, conv_w, conv_b, conv_b)


def _ml_gate_kernel(g_ref, b_ref, gt_ref, gg_ref, *, chunk):
    s = g_ref.shape[0]
    row = lax.broadcasted_iota(jnp.int32, (chunk, chunk), 0)
    col = lax.broadcasted_iota(jnp.int32, (chunk, chunk), 1)
    prefix = jnp.where(col <= row, 1.0, 0.0).astype(BF16)
    suffix = jnp.where(col >= row, 1.0, 0.0).astype(BF16)
    lane = lax.broadcasted_iota(jnp.int32, (chunk, LANES), 1)
    is_forget = ((lane % (2 * ML_HEADS)) >= ML_HEADS) & (lane < 4 * ML_HEADS)
    is_fwd = lane < 2 * ML_HEADS
    for r0 in range(0, s, chunk):
        g = g_ref[r0:r0 + chunk, :] + b_ref[...]
        lf = jnp.minimum(g, 0.0) - jnp.log(1.0 + jnp.exp(-jnp.abs(g)))
        parts = _split3(jnp.where(is_forget, lf, 0.0))
        cum_f = sum(jnp.dot(prefix, part, preferred_element_type=F32) for part in parts)
        cum_b = sum(jnp.dot(suffix, part, preferred_element_type=F32) for part in parts)
        out = jnp.where(is_forget, jnp.where(is_fwd, cum_f, cum_b), g)
        gt_ref[r0:r0 + chunk, :] = out
        gg_ref[:, r0:r0 + chunk] = out.T[0:4 * ML_HEADS, :]


def _ml_gates(g3, gate_b):
    nseq, s, _ = g3.shape
    return pl.pallas_call(
        functools.partial(_ml_gate_kernel, chunk=min(ML_CHUNK, s)),
        out_shape=(jax.ShapeDtypeStruct((nseq, s, LANES), F32),
                   jax.ShapeDtypeStruct((nseq, 4 * ML_HEADS, s), F32)),
        grid=(nseq,),
        in_specs=[
            pl.BlockSpec((None, s, LANES), lambda b: (b, 0, 0)),
            pl.BlockSpec((1, LANES), lambda b: (0, 0)),
        ],
        out_specs=(
            pl.BlockSpec((None, s, LANES), lambda b: (b, 0, 0)),
            pl.BlockSpec((None, 4 * ML_HEADS, s), lambda b: (b, 0, 0)),
        ),
        compiler_params=_cparams("arbitrary"),
        name="ml_gates",
    )(g3, gate_b)


def _ml_scan_kernel(*refs, nb, reverse, post):
    if post:
        (q_ref, kt_ref, v_ref, gt_ref, gg_ref, c0_ref, m0_ref, hp_ref, og_ref, gain_ref,
         h_ref, c_out_ref, m_out_ref, c_sc, m_sc) = refs
    else:
        (q_ref, kt_ref, v_ref, gt_ref, gg_ref, c0_ref, m0_ref,
         h_ref, c_out_ref, m_out_ref, c_sc, m_sc) = refs
    step = pl.program_id(0)
    L = q_ref.shape[1]
    nh = ML_HEADS

    @pl.when(step == 0)
    def _():
        c_sc[...] = c0_ref[...]
        m_sc[...] = m0_ref[...]

    row = lax.broadcasted_iota(jnp.int32, (L, L), 0)
    col = lax.broadcasted_iota(jnp.int32, (L, L), 1)
    incl = (col >= row) if reverse else (col <= row)
    last = 0 if reverse else L - 1

    def lanes(h):
        ig_lane = (2 * nh if reverse else 0) + h
        return ig_lane, ig_lane + nh

    col_rep = {}
    for b in range(nb):
        gt = gt_ref[b]
        for h in range(nh):
            ig_lane, lf_lane = lanes(h)
            col_rep[b, h] = (jnp.broadcast_to(gt[:, lf_lane:lf_lane + 1], (L, LANES)),
                             jnp.broadcast_to(gt[:, ig_lane:ig_lane + 1], (L, LANES)))

    for b in range(nb):
        gg = gg_ref[b]
        for h in range(nh):
            bh = b * nh + h
            ig_lane, lf_lane = lanes(h)
            hs = slice(h * ML_DK, (h + 1) * ML_DK)
            q = q_ref[b, :, hs]
            kt = kt_ref[b, hs, :]
            v = v_ref[b, :, hs]
            c_aug = c_sc[bh]
            m_prev = m_sc[bh][0:1, :]

            bcum, ig = col_rep[b, h]
            bcum_row = gg[lf_lane:lf_lane + 1, :]
            ig_row = gg[ig_lane:ig_lane + 1, :]
            total = bcum[last:last + 1, :]

            log_d = jnp.where(incl, bcum[:, :L] - bcum_row + ig_row, -jnp.inf)
            log_prev = bcum + m_prev
            m_t = jnp.maximum(log_prev, jnp.max(log_d, axis=-1, keepdims=True))
            dmat = jnp.exp(log_d - m_t[:, :L])
            w_prev = jnp.exp(log_prev - m_t)
            s = jnp.dot(q, kt, preferred_element_type=F32) * dmat
            lhs = jnp.concatenate([s.astype(BF16), (w_prev * q.astype(F32)).astype(BF16)], axis=-1)
            rhs = jnp.concatenate([jnp.concatenate([v, jnp.ones_like(v)], axis=-1), c_aug.astype(BF16)], axis=0)
            tot = jnp.dot(lhs, rhs, preferred_element_type=F32)
            num = tot[:, :ML_DV]
            den = tot[:, ML_DV:]
            hv = num / jnp.maximum(jnp.abs(den), jnp.exp(-m_t))

            m_new = m_t[last:last + 1, :]
            w_s = jnp.exp(total - bcum + ig - m_new)
            decay = jnp.exp(total + m_prev - m_new)
            wv = jnp.concatenate([w_s * v.astype(F32), w_s], axis=-1).astype(BF16)
            kv = jnp.dot(kt, wv, preferred_element_type=F32)
            c_sc[bh] = jnp.concatenate([decay, decay], axis=-1) * c_aug + kv
            m_sc[bh] = jnp.broadcast_to(m_new, m_sc.shape[1:])

            if post:
                hv = hv + hp_ref[b, :, hs]
                hv = hv * lax.rsqrt(jnp.mean(hv * hv, axis=-1, keepdims=True) + EPS) * gain_ref[:, hs]
                og = og_ref[b, :, hs].astype(F32)
                hv = hv / (1.0 + jnp.exp(-og))
            h_ref[b, :, hs] = hv.astype(h_ref.dtype)

    @pl.when(step == pl.num_programs(0) - 1)
    def _():
        c_out_ref[...] = c_sc[...]
        m_out_ref[...] = m_sc[...]


def _ml_scan(q3, kt3, p3, gt3, gg3, state, reverse, post_args=None):
    nb, s, _ = q3.shape
    assert ML_DK == LANES and ML_DV == LANES and min(ML_CHUNK, s) <= LANES
    L = min(ML_CHUNK, s)
    nc = s // L
    w = ML_HEADS * ML_DK
    c0, m0 = state
    post = post_args is not None
    cidx = (lambda c: nc - 1 - c) if reverse else (lambda c: c)
    in_specs = [
        pl.BlockSpec((nb, L, w), lambda c: (0, cidx(c), 0)),
        pl.BlockSpec((nb, w, L), lambda c: (0, 0, cidx(c))),
        pl.BlockSpec((nb, L, w), lambda c: (0, cidx(c), C_MLV // w)),
        pl.BlockSpec((nb, L, LANES), lambda c: (0, cidx(c), 0)),
        pl.BlockSpec((nb, 4 * ML_HEADS, L), lambda c: (0, 0, cidx(c))),
        pl.BlockSpec(c0.shape, lambda c: (0, 0, 0)),
        pl.BlockSpec(m0.shape, lambda c: (0, 0, 0)),
    ]
    args = [q3, kt3, p3, gt3, gg3, c0, m0]
    if post:
        h_prev, gain = post_args
        in_specs += [
            pl.BlockSpec((nb, L, w), lambda c: (0, cidx(c), 0)),
            pl.BlockSpec((nb, L, w), lambda c: (0, cidx(c), C_MLO // w)),
            pl.BlockSpec((1, w), lambda c: (0, 0)),
        ]
        args += [h_prev, p3, gain]
    return pl.pallas_call(
        functools.partial(_ml_scan_kernel, nb=nb, reverse=reverse, post=post),
        out_shape=(jax.ShapeDtypeStruct((nb, s, w), BF16 if post else F32),
                   jax.ShapeDtypeStruct(c0.shape, F32), jax.ShapeDtypeStruct(m0.shape, F32)),
        grid=(nc,),
        in_specs=in_specs,
        out_specs=(
            pl.BlockSpec((nb, L, w), lambda c: (0, cidx(c), 0)),
            pl.BlockSpec(c0.shape, lambda c: (0, 0, 0)),
            pl.BlockSpec(m0.shape, lambda c: (0, 0, 0)),
        ),
        scratch_shapes=[pltpu.VMEM(c0.shape, F32), pltpu.VMEM(m0.shape, F32)],
        compiler_params=_cparams("arbitrary"),
        name="ml_scan",
    )(*args)


def _out_proj_kernel(da_ref, ml_ref, mla_ref, w_ref, x_ref, gate_ref, h_ref):
    n_da = da_ref.shape[1]
    n_ml = ml_ref.shape[1]
    acc = jnp.dot(da_ref[...], w_ref[0:n_da, :], preferred_element_type=F32)
    acc += jnp.dot(ml_ref[...], w_ref[n_da:n_da + n_ml, :], preferred_element_type=F32)
    acc += jnp.dot(mla_ref[...], w_ref[n_da + n_ml:, :], preferred_element_type=F32)
    h_ref[...] = x_ref[...] + gate_ref[...] * acc


def _out_proj(da, ml, mla, w, layer, x, mod, rows_per_mod, mod_base):
    r, d = x.shape
    tm = _pick(min(r, rows_per_mod), 1024)
    tn = _pick(d, 1024)
    nt = d // tn
    return pl.pallas_call(
        _out_proj_kernel,
        out_shape=jax.ShapeDtypeStruct((r, d), F32),
        grid=(nt, r // tm),
        in_specs=[
            pl.BlockSpec((tm, da.shape[1]), lambda j, i: (i, 0)),
            pl.BlockSpec((tm, ml.shape[1]), lambda j, i: (i, 0)),
            pl.BlockSpec((tm, mla.shape[1]), lambda j, i: (i, 0)),
            pl.BlockSpec((None, w.shape[1], tn), lambda j, i: (layer, 0, j)),
            pl.BlockSpec((tm, tn), lambda j, i: (i, j)),
            pl.BlockSpec((None, 1, tn), lambda j, i: (i * tm // rows_per_mod + mod_base, 0, 2 * nt + j)),
        ],
        out_specs=pl.BlockSpec((tm, tn), lambda j, i: (i, j)),
        compiler_params=_cparams("arbitrary", "arbitrary"),
        name="out_proj",
    )(da, ml, mla, w, x, mod)


def _ffn_kernel(h_ref, mod_ref, g_ref, wg_ref, wu_ref, wd_ref, o_ref, hn_ref, *, d, rc):
    j = pl.program_id(1)

    @pl.when(j == 0)
    def _():
        shift = mod_ref[:, 3 * d:4 * d]
        scale1 = 1.0 + mod_ref[:, 4 * d:5 * d]
        _norm_rows(h_ref, g_ref, shift, scale1, hn_ref, rc)
        o_ref[...] = h_ref[...]

    hn = hn_ref[...]
    gate = jnp.dot(hn, wg_ref[...], preferred_element_type=F32)
    up = jnp.dot(hn, wu_ref[...], preferred_element_type=F32)
    act = (gate / (1.0 + jnp.exp(-gate)) * up).astype(BF16)
    o_ref[...] += mod_ref[:, 5 * d:6 * d] * jnp.dot(act, wd_ref[...], preferred_element_type=F32)


def _ffn(h, mod, g, w_gu, w_down, layer, rows_per_mod, mod_base):
    r, d = h.shape
    ff = w_down.shape[1]
    tm = _pick(min(r, rows_per_mod), 1024)
    th = 512 if ff % 512 == 0 else 256
    nh = ff // th
    return pl.pallas_call(
        functools.partial(_ffn_kernel, d=d, rc=_pick(tm, 64)),
        out_shape=jax.ShapeDtypeStruct((r, d), F32),
        grid=(r // tm, nh),
        in_specs=[
            pl.BlockSpec((tm, d), lambda i, j: (i, 0)),
            pl.BlockSpec((None, 1, 6 * d), lambda i, j: (i * tm // rows_per_mod + mod_base, 0, 0)),
            pl.BlockSpec((1, d), lambda i, j: (0, 0)),
            pl.BlockSpec((None, d, th), lambda i, j: (layer, 0, j)),
            pl.BlockSpec((None, d, th), lambda i, j: (layer, 0, nh + j)),
            pl.BlockSpec((None, th, d), lambda i, j: (layer, j, 0)),
        ],
        out_specs=pl.BlockSpec((tm, d), lambda i, j: (i, 0)),
        scratch_shapes=[pltpu.VMEM((tm, d), BF16)],
        compiler_params=_cparams("arbitrary", "arbitrary"),
        name="ffn",
    )(h, mod, g, w_gu, w_gu, w_down)


def _rope_tables(n_lat):
    pos = jnp.arange(n_lat, dtype=jnp.int32)
    r = (pos // GRID_W).astype(F32)
    c = (pos % GRID_W).astype(F32)
    half = ROPE_DIM // 4
    inv = ROPE_BASE ** (-jnp.arange(half, dtype=F32) / half)
    ar = r[:, None] * inv
    ac = c[:, None] * inv
    cos64 = jnp.concatenate([jnp.cos(ar), jnp.cos(ar), jnp.cos(ac), jnp.cos(ac)], axis=-1)
    sin64 = jnp.concatenate([-jnp.sin(ar), jnp.sin(ar), -jnp.sin(ac), jnp.sin(ac)], axis=-1)
    return cos64, sin64


def _pad_cols(a, width):
    return jnp.pad(a, ((0, 0),) * (a.ndim - 1) + ((0, width - a.shape[-1]),))


def _layout_w_in(w):
    gates0 = C_CQ
    gates1 = gates0 + 4 * ML_HEADS
    main = w[..., :gates0]
    gates = w[..., gates0:gates1]
    rest = w[..., gates1:]
    n_cq_ckv = C_KPE - C_CQ
    cq_ckv = rest[..., :n_cq_ckv]
    kpe = rest[..., n_cq_ckv:]
    return jnp.concatenate([main, cq_ckv, _pad_cols(kpe, LANES), _pad_cols(gates, LANES)], axis=-1)


def kernel(x, c, ctx, c_ctx, mod_w, mod_b, norm1_g, norm2_g, w_in, da_qk_g, da_lambda, da_out_g,
           ml_conv_w, ml_conv_b, ml_gate_b, ml_out_g, mla_q_norm_g, mla_kv_norm_g, mla_w_uq, mla_w_ukv,
           mla_qk_g, w_out, ffn_w_gu, ffn_w_down):
    batch, seq, d = x.shape
    ctx_len = ctx.shape[1]
    depth = mod_w.shape[0]
    in_dtype = x.dtype
    assert w_in.shape[2] == C_CQ + 4 * ML_HEADS + (C_KPE - C_CQ) + MLA_ROPE

    n_mod_rows = -(-(batch + 1) // 8) * 8
    cc = jnp.concatenate([c, c_ctx[None, :], jnp.zeros((n_mod_rows - batch - 1, d), F32)], axis=0)
    mod_all = _modulation(cc, mod_w, mod_b)

    cos64, sin64 = _rope_tables(seq)
    cos_da = jnp.tile(cos64, (1, 2))
    sin_da = jnp.tile(sin64, (1, 2))
    cos_mla = jnp.concatenate([cos64, jnp.ones_like(cos64)], axis=-1)
    sin_mla = jnp.concatenate([sin64, jnp.zeros_like(sin64)], axis=-1)
    tab_rows = _pick(batch * ctx_len, 1024)
    cos_id = jnp.ones((tab_rows, LANES), F32)
    sin_id = jnp.zeros((tab_rows, LANES), F32)
    lane = jnp.arange(LANES)
    group_mean = jnp.where((lane[:, None] // DA_HALF) == (lane[None, :] // DA_HALF),
                           1.0 / DA_HALF, 0.0).astype(BF16)

    w_in_b = _layout_w_in(w_in.astype(BF16))
    w_out_b = w_out.astype(BF16)
    w_gu_b = ffn_w_gu.astype(BF16)
    w_down_b = ffn_w_down.astype(BF16)
    wq_b = jnp.pad(mla_w_uq.reshape(depth, -1, MLA_HEADS, MLA_QK),
                   ((0, 0), (0, 0), (0, 0), (0, MLA_PAD - MLA_QK))).reshape(depth, -1, MLA_HEADS * MLA_PAD).astype(BF16)
    wkv_b = mla_w_ukv.astype(BF16)

    xl = x.reshape(batch * seq, d)
    xc = ctx.reshape(batch * ctx_len, d)
    huge = batch * max(seq, ctx_len) * 2

    for l in range(depth):
        need_ctx = l < depth - 1
        lam_init = 0.8 - 0.6 * math.exp(-0.3 * l)
        mod = mod_all[l].reshape(n_mod_rows, 1, 6 * d)
        g1 = norm1_g[l].reshape(1, d)
        g2 = norm2_g[l].reshape(1, d)
        gq_da = jnp.tile(da_qk_g[l, 0], 2).reshape(1, LANES)
        gk_da = jnp.tile(da_qk_g[l, 1], 2).reshape(1, LANES)
        gq_mla = _pad_cols(mla_qk_g[l, 0].reshape(1, MLA_QK), MLA_PAD)
        gk_mla = _pad_cols(mla_qk_g[l, 1].reshape(1, MLA_QK), MLA_PAD)
        gcq = mla_q_norm_g[l].reshape(1, -1)
        gckv = mla_kv_norm_g[l].reshape(1, -1)
        gate_b = _pad_cols(ml_gate_b[l].reshape(1, -1), LANES)
        conv_w = ml_conv_w[l]
        conv_b = ml_conv_b[l].reshape(1, -1)
        ml_gain = ml_out_g[l].reshape(1, -1)
        da_gain = da_out_g[l].reshape(1, -1)
        lam_p = da_lambda[l]

        p_l, g_l = _in_proj(xl, mod, g1, w_in_b, l, seq, 0)
        p_c, g_c = _in_proj(xc, mod, g1, w_in_b, l, huge, batch)

        qd_l, kd_l = _da_prep(p_l, cos_da, sin_da, gq_da, gk_da, group_mean)
        qd_c, kd_c = _da_prep(p_c, cos_id, sin_id, gq_da, gk_da, group_mean)
        da_l = _attention("da", qd_l, [(kd_l, p_l, C_DAV), (kd_c, p_c, C_DAV)], batch,
                          [lam_p, da_gain], lam_init)
        qm_l, km_l, vm_l = _mla_prep(p_l, cos_mla, sin_mla, gcq, gckv, gq_mla, gk_mla, wq_b, wkv_b, l)
        qm_c, km_c, vm_c = _mla_prep(p_c, cos_id, sin_id, gcq, gckv, gq_mla, gk_mla, wq_b, wkv_b, l)
        mla_l = _attention("mla", qm_l, [(km_l, vm_l, 0), (km_c, vm_c, 0)], batch, [])

        p3_l = p_l.reshape(batch, seq, P_COLS)
        p3_c = p_c.reshape(batch, ctx_len, P_COLS)
        q_l, kt_l = _ml_conv(p3_l, conv_w, conv_b)
        q_c, kt_c = _ml_conv(p3_c, conv_w, conv_b)
        gt_l, gg_l = _ml_gates(g_l.reshape(batch, seq, LANES), gate_b)
        gt_c, gg_c = _ml_gates(g_c.reshape(batch, ctx_len, LANES), gate_b)
        s0 = (jnp.zeros((batch * ML_HEADS, ML_DK, 2 * ML_DV), F32),
              jnp.zeros((batch * ML_HEADS, 8, LANES), F32))
        h_cf, c_f, m_f = _ml_scan(q_c, kt_c, p3_c, gt_c, gg_c, s0, False)
        h_lf, _, _ = _ml_scan(q_l, kt_l, p3_l, gt_l, gg_l, (c_f, m_f), False)
        if need_ctx:
            ml_c, c_b, m_b = _ml_scan(q_c, kt_c, p3_c, gt_c, gg_c, s0, True, (h_cf, ml_gain))
        else:
            _, c_b, m_b = _ml_scan(q_c, kt_c, p3_c, gt_c, gg_c, s0, True)
        ml_l, _, _ = _ml_scan(q_l, kt_l, p3_l, gt_l, gg_l, (c_b, m_b), True, (h_lf, ml_gain))

        h_l = _out_proj(da_l, ml_l.reshape(batch * seq, -1), mla_l, w_out_b, l, xl, mod, seq, 0)
        xl = _ffn(h_l, mod, g2, w_gu_b, w_down_b, l, seq, 0)
        if need_ctx:
            da_c = _attention("da", qd_c, [(kd_c, p_c, C_DAV)], batch, [lam_p, da_gain], lam_init)
            mla_c = _attention("mla", qm_c, [(km_c, vm_c, 0)], batch, [])
            h_c = _out_proj(da_c, ml_c.reshape(batch * ctx_len, -1), mla_c, w_out_b, l, xc, mod, huge, batch)
            xc = _ffn(h_c, mod, g2, w_gu_b, w_down_b, l, huge, batch)

    return xl.reshape(batch, seq, d).astype(in_dtype)
```

```python
import functools
import math

import jax
import jax.numpy as jnp
from jax import lax
from jax.experimental import pallas as pl
from jax.experimental.pallas import tpu as pltpu

F32 = jnp.float32
BF16 = jnp.bfloat16

DA_HEADS = 4
DA_HALF = 64
DA_V = 2 * DA_HALF
ML_HEADS = 4
ML_DK = 128
ML_DV = 128
MLA_HEADS = 8
MLA_NOPE = 128
MLA_ROPE = 64
MLA_V = 128
MLA_QK = MLA_NOPE + MLA_ROPE
MLA_PAD = 256
GRID_W = 64
ROPE_DIM = 64
ROPE_BASE = 10000.0
EPS = 1e-6
LOG2E = 1.4426950408889634

LANES = 128
ML_CHUNK = 128
VMEM_LIMIT = 56 * 1024 * 1024

C_DAQ, C_DAK, C_DAV = 0, 512, 1024
C_MLQ, C_MLK, C_MLV, C_MLO = 1536, 2048, 2560, 3072
C_CQ, C_CKV, C_KPE, C_GATE = 3584, 4096, 4352, 4480
P_COLS = 4608


def _cparams(*sem):
    return pltpu.CompilerParams(dimension_semantics=sem, vmem_limit_bytes=VMEM_LIMIT)


def _pick(n, pref):
    if n <= pref:
        return n
    t = pref
    while n % t:
        t //= 2
    return t


def _mod_kernel(c_ref, w_ref, b_ref, o_ref):
    c = c_ref[...]
    s = (c / (1.0 + jnp.exp(-c))).astype(BF16)
    o_ref[...] = jnp.dot(s, w_ref[...].astype(BF16), preferred_element_type=F32) + b_ref[...]


def _modulation(cc, mod_w, mod_b):
    depth, d, n = mod_w.shape
    tn = _pick(n, 1024)
    return pl.pallas_call(
        _mod_kernel,
        out_shape=jax.ShapeDtypeStruct((depth, cc.shape[0], n), F32),
        grid=(depth, n // tn),
        in_specs=[
            pl.BlockSpec((cc.shape[0], d), lambda l, j: (0, 0)),
            pl.BlockSpec((None, d, tn), lambda l, j: (l, 0, j)),
            pl.BlockSpec((None, 1, tn), lambda l, j: (l, 0, j)),
        ],
        out_specs=pl.BlockSpec((None, cc.shape[0], tn), lambda l, j: (l, 0, j)),
        compiler_params=_cparams("arbitrary", "arbitrary"),
        name="modulation",
    )(cc, mod_w, mod_b.reshape(depth, 1, n))


def _norm_rows(x_ref, g_ref, shift, scale1, out_ref, rc):
    tm = x_ref.shape[0]
    gs = g_ref[...] * scale1

    def body(r, carry):
        rows = pl.ds(pl.multiple_of(r * rc, rc), rc)
        x = x_ref[rows, :]
        ms = jnp.mean(x * x, axis=-1, keepdims=True)
        out_ref[rows, :] = (x * lax.rsqrt(ms + EPS) * gs + shift).astype(out_ref.dtype)
        return carry

    lax.fori_loop(0, tm // rc, body, 0, unroll=2)


def _in_proj_kernel(x_ref, mod_ref, g_ref, w_ref, p_ref, gate_ref, xn_ref, *, d, rc):
    n = pl.program_id(1)

    @pl.when(n == 0)
    def _():
        shift = mod_ref[:, 0:d]
        scale1 = 1.0 + mod_ref[:, d:2 * d]
        _norm_rows(x_ref, g_ref, shift, scale1, xn_ref, rc)

    acc = jnp.dot(xn_ref[...], w_ref[...], preferred_element_type=F32)
    p_ref[...] = acc.astype(p_ref.dtype)

    @pl.when(n == pl.num_programs(1) - 1)
    def _():
        gate_ref[...] = acc[:, acc.shape[1] - LANES:]


def _in_proj(x, mod, g, w, layer, rows_per_mod, mod_base):
    r, d = x.shape
    n = w.shape[2]
    tm = _pick(min(r, rows_per_mod), 1024)
    tn = 1536
    assert n % tn == 0
    return pl.pallas_call(
        functools.partial(_in_proj_kernel, d=d, rc=_pick(tm, 64)),
        out_shape=(jax.ShapeDtypeStruct((r, n), BF16), jax.ShapeDtypeStruct((r, LANES), F32)),
        grid=(r // tm, n // tn),
        in_specs=[
            pl.BlockSpec((tm, d), lambda i, j: (i, 0)),
            pl.BlockSpec((None, 1, 6 * d), lambda i, j: (i * tm // rows_per_mod + mod_base, 0, 0)),
            pl.BlockSpec((1, d), lambda i, j: (0, 0)),
            pl.BlockSpec((None, d, tn), lambda i, j: (layer, 0, j)),
        ],
        out_specs=(
            pl.BlockSpec((tm, tn), lambda i, j: (i, j)),
            pl.BlockSpec((tm, LANES), lambda i, j: (i, 0)),
        ),
        scratch_shapes=[pltpu.VMEM((tm, d), BF16)],
        compiler_params=_cparams("arbitrary", "arbitrary"),
        name="in_proj",
    )(x, mod, g, w)


def _swap16(x):
    lane = lax.broadcasted_iota(jnp.int32, x.shape, 1)
    fwd = pltpu.roll(x, LANES - 16, 1)
    bwd = pltpu.roll(x, 16, 1)
    return jnp.where((lane % 32) < 16, fwd, bwd)


def _rope(x, cos, sin):
    return x * cos + _swap16(x) * sin


def _split3(x):
    hi = x.astype(BF16)
    r1 = x - hi.astype(F32)
    mid = r1.astype(BF16)
    lo = (r1 - mid.astype(F32)).astype(BF16)
    return hi, mid, lo


def _group_mean_sq(x, gm):
    x2 = x * x
    hi = x2.astype(BF16)
    lo = (x2 - hi.astype(F32)).astype(BF16)
    return jnp.dot(hi, gm, preferred_element_type=F32) + jnp.dot(lo, gm, preferred_element_type=F32)


def _da_prep_kernel(q_ref, k_ref, cos_ref, sin_ref, gq_ref, gk_ref, gm_ref, qo_ref, ko_ref, *, qscale):
    cos = cos_ref[...]
    sin = sin_ref[...]
    gm = gm_ref[...]
    lane = lax.broadcasted_iota(jnp.int32, cos.shape, 1)
    first = lane < DA_HALF
    for h in range(DA_HEADS):
        sl = slice(h * LANES, (h + 1) * LANES)
        q = q_ref[:, sl].astype(F32)
        q = q * lax.rsqrt(_group_mean_sq(q, gm) + EPS) * gq_ref[...]
        q = _rope(q, cos, sin) * qscale
        qo_ref[:, 2 * h * LANES:(2 * h + 1) * LANES] = jnp.where(first, q, 0.0).astype(qo_ref.dtype)
        qo_ref[:, (2 * h + 1) * LANES:(2 * h + 2) * LANES] = jnp.where(first, 0.0, q).astype(qo_ref.dtype)
        k = k_ref[:, sl].astype(F32)
        k = k * lax.rsqrt(_group_mean_sq(k, gm) + EPS) * gk_ref[...]
        ko_ref[:, sl] = _rope(k, cos, sin).astype(ko_ref.dtype)


def _da_prep(p, cos, sin, gq, gk, gm):
    r = p.shape[0]
    tm = _pick(min(r, cos.shape[0]), 1024)
    tab_blocks = cos.shape[0] // tm
    w = DA_HEADS * LANES
    qscale = DA_HALF ** -0.5 * LOG2E
    return pl.pallas_call(
        functools.partial(_da_prep_kernel, qscale=qscale),
        out_shape=(jax.ShapeDtypeStruct((r, 2 * w), BF16), jax.ShapeDtypeStruct((r, w), BF16)),
        grid=(r // tm,),
        in_specs=[
            pl.BlockSpec((tm, w), lambda i: (i, C_DAQ // w)),
            pl.BlockSpec((tm, w), lambda i: (i, C_DAK // w)),
            pl.BlockSpec((tm, LANES), lambda i: (i % tab_blocks, 0)),
            pl.BlockSpec((tm, LANES), lambda i: (i % tab_blocks, 0)),
            pl.BlockSpec((1, LANES), lambda i: (0, 0)),
            pl.BlockSpec((1, LANES), lambda i: (0, 0)),
            pl.BlockSpec((LANES, LANES), lambda i: (0, 0)),
        ],
        out_specs=(
            pl.BlockSpec((tm, 2 * w), lambda i: (i, 0)),
            pl.BlockSpec((tm, w), lambda i: (i, 0)),
        ),
        compiler_params=_cparams("arbitrary"),
        name="da_prep",
    )(p, p, cos, sin, gq, gk, gm)


def _mla_prep_kernel(cq_ref, ckv_ref, kpe_ref, cos_ref, sin_ref, gcq_ref, gckv_ref, gq_ref, gk_ref,
                     wq_ref, wkv_ref, qo_ref, ko_ref, vo_ref, *, qscale):
    cos = cos_ref[...]
    sin = sin_ref[...]
    lane = lax.broadcasted_iota(jnp.int32, cos.shape, 1)
    inv_qk = 1.0 / MLA_QK

    cq = cq_ref[...].astype(F32)
    cq = cq * lax.rsqrt(jnp.mean(cq * cq, axis=-1, keepdims=True) + EPS) * gcq_ref[...]
    qf = jnp.dot(cq.astype(BF16), wq_ref[...], preferred_element_type=F32)

    ckv = ckv_ref[...].astype(F32)
    ckv = ckv * lax.rsqrt(jnp.mean(ckv * ckv, axis=-1, keepdims=True) + EPS) * gckv_ref[...]
    kvf = jnp.dot(ckv.astype(BF16), wkv_ref[...], preferred_element_type=F32)

    kpe = jnp.where(lane < MLA_ROPE, kpe_ref[...].astype(F32), 0.0)
    kpe_sq = kpe * kpe
    kpe_rot = _rope(kpe * gk_ref[:, LANES:], cos, sin)

    for h in range(MLA_HEADS):
        q1 = qf[:, h * MLA_PAD:h * MLA_PAD + LANES]
        q2 = qf[:, h * MLA_PAD + LANES:(h + 1) * MLA_PAD]
        ms = jnp.sum(q1 * q1 + q2 * q2, axis=-1, keepdims=True) * inv_qk
        rq = lax.rsqrt(ms + EPS) * qscale
        qo_ref[:, h * MLA_PAD:h * MLA_PAD + LANES] = (q1 * rq * gq_ref[:, :LANES]).astype(qo_ref.dtype)
        q2 = _rope(q2 * rq * gq_ref[:, LANES:], cos, sin)
        qo_ref[:, h * MLA_PAD + LANES:(h + 1) * MLA_PAD] = q2.astype(qo_ref.dtype)

        kn = kvf[:, h * MLA_PAD:h * MLA_PAD + LANES]
        ms = jnp.sum(kn * kn + kpe_sq, axis=-1, keepdims=True) * inv_qk
        rk = lax.rsqrt(ms + EPS)
        ko_ref[:, h * MLA_PAD:h * MLA_PAD + LANES] = (kn * rk * gk_ref[:, :LANES]).astype(ko_ref.dtype)
        ko_ref[:, h * MLA_PAD + LANES:(h + 1) * MLA_PAD] = (kpe_rot * rk).astype(ko_ref.dtype)
        vo_ref[:, h * MLA_V:(h + 1) * MLA_V] = kvf[:, h * MLA_PAD + LANES:(h + 1) * MLA_PAD].astype(vo_ref.dtype)


def _mla_prep(p, cos, sin, gcq, gckv, gq, gk, wq, wkv, layer):
    r = p.shape[0]
    tm = _pick(min(r, cos.shape[0]), 1024)
    tab_blocks = cos.shape[0] // tm
    q_rank = wq.shape[1]
    kv_rank = wkv.shape[1]
    wide = MLA_HEADS * MLA_PAD
    qscale = MLA_QK ** -0.5 * LOG2E
    const = lambda i: (0, 0)
    return pl.pallas_call(
        functools.partial(_mla_prep_kernel, qscale=qscale),
        out_shape=(jax.ShapeDtypeStruct((r, wide), BF16), jax.ShapeDtypeStruct((r, wide), BF16),
                   jax.ShapeDtypeStruct((r, MLA_HEADS * MLA_V), BF16)),
        grid=(r // tm,),
        in_specs=[
            pl.BlockSpec((tm, q_rank), lambda i: (i, C_CQ // q_rank)),
            pl.BlockSpec((tm, kv_rank), lambda i: (i, C_CKV // kv_rank)),
            pl.BlockSpec((tm, LANES), lambda i: (i, C_KPE // LANES)),
            pl.BlockSpec((tm, LANES), lambda i: (i % tab_blocks, 0)),
            pl.BlockSpec((tm, LANES), lambda i: (i % tab_blocks, 0)),
            pl.BlockSpec((1, q_rank), const),
            pl.BlockSpec((1, kv_rank), const),
            pl.BlockSpec((1, MLA_PAD), const),
            pl.BlockSpec((1, MLA_PAD), const),
            pl.BlockSpec((None, q_rank, wide), lambda i: (layer, 0, 0)),
            pl.BlockSpec((None, kv_rank, wide), lambda i: (layer, 0, 0)),
        ],
        out_specs=(
            pl.BlockSpec((tm, wide), lambda i: (i, 0)),
            pl.BlockSpec((tm, wide), lambda i: (i, 0)),
            pl.BlockSpec((tm, MLA_HEADS * MLA_V), lambda i: (i, 0)),
        ),
        compiler_params=_cparams("arbitrary"),
        name="mla_prep",
    )(p, p, p, cos, sin, gcq, gckv, gq, gk, wq, wkv)


def _attn_core(q_ref, k_refs, v_refs, n_comp, dq, kc):
    chunks = []
    for ri, kr in enumerate(k_refs):
        rows = kr.shape[0]
        step = min(kc, rows)
        for r0 in range(0, rows, step):
            chunks.append((ri, r0, step))
    outs = []
    for c in range(n_comp):
        q = q_ref[:, c * dq:(c + 1) * dq]
        m = acc = None
        for ri, r0, step in chunks:
            s = lax.dot_general(q, k_refs[ri][r0:r0 + step, :], (((1,), (1,)), ((), ())),
                                preferred_element_type=F32)
            cm = jnp.max(s, axis=-1, keepdims=True)
            m_new = cm if m is None else jnp.maximum(m, cm)
            p = jnp.exp2(s - m_new)
            vc = v_refs[ri][r0:r0 + step, :]
            pv = jnp.dot(p.astype(BF16), jnp.concatenate([vc, jnp.ones_like(vc)], axis=-1),
                         preferred_element_type=F32)
            acc = pv if m is None else jnp.exp2(m - m_new) * acc + pv
            m = m_new
        dv = acc.shape[1] // 2
        outs.append(acc[:, :dv] / acc[:, dv:])
    return outs


def _da_attn_kernel(*refs, n_kv, lam_init, kc):
    q_ref = refs[0]
    k_refs = refs[1:1 + n_kv]
    v_refs = refs[1 + n_kv:1 + 2 * n_kv]
    lam_ref, g_ref, o_ref = refs[1 + 2 * n_kv:]
    lp = lam_ref[...]
    lam = (jnp.exp(jnp.sum(lp[0:1] * lp[1:2], axis=-1, keepdims=True))
           - jnp.exp(jnp.sum(lp[2:3] * lp[3:4], axis=-1, keepdims=True)) + lam_init)
    o0, o1 = _attn_core(q_ref, k_refs, v_refs, 2, LANES, kc)
    o = o0 - lam * o1
    o = o * lax.rsqrt(jnp.mean(o * o, axis=-1, keepdims=True) + EPS) * g_ref[...]
    o_ref[...] = (o * (1.0 - lam_init)).astype(o_ref.dtype)


def _mla_attn_kernel(*refs, n_kv, kc):
    q_ref = refs[0]
    k_refs = refs[1:1 + n_kv]
    v_refs = refs[1 + n_kv:1 + 2 * n_kv]
    (o_ref,) = refs[1 + 2 * n_kv:]
    (o,) = _attn_core(q_ref, k_refs, v_refs, 1, MLA_PAD, kc)
    o_ref[...] = o.astype(o_ref.dtype)


def _attention(kind, q, kvs, batch, extra, lam_init=0.0):
    heads = DA_HEADS if kind == "da" else MLA_HEADS
    qw = 2 * LANES if kind == "da" else MLA_PAD
    dq = LANES if kind == "da" else MLA_PAD
    dv = DA_V if kind == "da" else MLA_V
    rq = q.shape[0]
    tq = _pick(rq // batch, 2048)
    nq = rq // batch // tq
    kc = 256
    k_specs, v_specs, k_args, v_args = [], [], [], []
    for k, v, vcol in kvs:
        kl = k.shape[0] // batch
        k_specs.append(pl.BlockSpec((kl, dq), lambda b, h, i: (b, h)))
        v_specs.append(pl.BlockSpec((kl, dv), lambda b, h, i, vcol=vcol: (b, vcol // dv + h)))
        k_args.append(k)
        v_args.append(v)
    extra_specs = [pl.BlockSpec(e.shape, lambda b, h, i: (0, 0)) for e in extra]
    if kind == "da":
        body = functools.partial(_da_attn_kernel, n_kv=len(kvs), lam_init=lam_init, kc=kc)
    else:
        body = functools.partial(_mla_attn_kernel, n_kv=len(kvs), kc=kc)
    return pl.pallas_call(
        body,
        out_shape=jax.ShapeDtypeStruct((rq, heads * dv), BF16),
        grid=(batch, heads, nq),
        in_specs=[pl.BlockSpec((tq, qw), lambda b, h, i: (b * nq + i, h))] + k_specs + v_specs + extra_specs,
        out_specs=pl.BlockSpec((tq, dv), lambda b, h, i: (b * nq + i, h)),
        compiler_params=_cparams("arbitrary", "arbitrary", "arbitrary"),
        name=kind + "_attn",
    )(q, *k_args, *v_args, *extra)


def _ml_conv_kernel(xq_ref, xk_ref, wq_ref, wk_ref, bq_ref, bk_ref, q_ref, kt_ref, pad_ref, *, kscale, tc):
    s = xq_ref.shape[0]
    zeros = jnp.zeros((8, LANES), F32)
    pad_ref[0:8, :] = zeros
    pad_ref[s + 8:s + 16, :] = zeros

    def conv_silu(x_ref, w_ref, b_ref):
        x = x_ref[...].astype(F32)
        pad_ref[8:s + 8, :] = x
        prev = pad_ref[7:s + 7, :]
        nxt = pad_ref[9:s + 9, :]
        y = b_ref[...] + prev * w_ref[0:1, :] + x * w_ref[1:2, :] + nxt * w_ref[2:3, :]
        return y / (1.0 + jnp.exp(-y))

    q_ref[...] = conv_silu(xq_ref, wq_ref, bq_ref).astype(q_ref.dtype)
    k = conv_silu(xk_ref, wk_ref, bk_ref) * kscale
    for r0 in range(0, s, tc):
        kt_ref[:, r0:r0 + tc] = k[r0:r0 + tc, :].T.astype(kt_ref.dtype)


def _ml_conv(p3, conv_w, conv_b):
    nseq, s, _ = p3.shape
    w = ML_HEADS * ML_DK
    q0 = C_MLQ // LANES
    k0 = C_MLK // LANES
    return pl.pallas_call(
        functools.partial(_ml_conv_kernel, kscale=ML_DK ** -0.5, tc=_pick(s, 512)),
        out_shape=(jax.ShapeDtypeStruct((nseq, s, w), BF16), jax.ShapeDtypeStruct((nseq, w, s), BF16)),
        grid=(nseq, ML_HEADS),
        in_specs=[
            pl.BlockSpec((None, s, LANES), lambda b, h: (b, 0, q0 + h)),
            pl.BlockSpec((None, s, LANES), lambda b, h: (b, 0, k0 + h)),
            pl.BlockSpec((3, LANES), lambda b, h: (0, h)),
            pl.BlockSpec((3, LANES), lambda b, h: (0, ML_HEADS + h)),
            pl.BlockSpec((1, LANES), lambda b, h: (0, h)),
            pl.BlockSpec((1, LANES), lambda b, h: (0, ML_HEADS + h)),
        ],
        out_specs=(
            pl.BlockSpec((None, s, LANES), lambda b, h: (b, 0, h)),
            pl.BlockSpec((None, ML_DK, s), lambda b, h: (b, h, 0)),
        ),
        scratch_shapes=[pltpu.VMEM((s + 16, LANES), F32)],
        compiler_params=_cparams("arbitrary", "arbitrary"),
        name="ml_conv",
    )(p3, p3, conv_w, conv_w, conv_b, conv_b)


def _ml_gate_kernel(g_ref, b_ref, gt_ref, gg_ref, *, chunk):
    s = g_ref.shape[0]
    row = lax.broadcasted_iota(jnp.int32, (chunk, chunk), 0)
    col = lax.broadcasted_iota(jnp.int32, (chunk, chunk), 1)
    prefix = jnp.where(col <= row, 1.0, 0.0).astype(BF16)
    suffix = jnp.where(col >= row, 1.0, 0.0).astype(BF16)
    lane = lax.broadcasted_iota(jnp.int32, (chunk, LANES), 1)
    is_forget = ((lane % (2 * ML_HEADS)) >= ML_HEADS) & (lane < 4 * ML_HEADS)
    is_fwd = lane < 2 * ML_HEADS
    for r0 in range(0, s, chunk):
        g = g_ref[r0:r0 + chunk, :] + b_ref[...]
        lf = jnp.minimum(g, 0.0) - jnp.log(1.0 + jnp.exp(-jnp.abs(g)))
        parts = _split3(jnp.where(is_forget, lf, 0.0))
        cum_f = sum(jnp.dot(prefix, part, preferred_element_type=F32) for part in parts)
        cum_b = sum(jnp.dot(suffix, part, preferred_element_type=F32) for part in parts)
        out = jnp.where(is_forget, jnp.where(is_fwd, cum_f, cum_b), g)
        gt_ref[r0:r0 + chunk, :] = out
        gg_ref[:, r0:r0 + chunk] = out.T[0:4 * ML_HEADS, :]


def _ml_gates(g3, gate_b):
    nseq, s, _ = g3.shape
    return pl.pallas_call(
        functools.partial(_ml_gate_kernel, chunk=min(ML_CHUNK, s)),
        out_shape=(jax.ShapeDtypeStruct((nseq, s, LANES), F32),
                   jax.ShapeDtypeStruct((nseq, 4 * ML_HEADS, s), F32)),
        grid=(nseq,),
        in_specs=[
            pl.BlockSpec((None, s, LANES), lambda b: (b, 0, 0)),
            pl.BlockSpec((1, LANES), lambda b: (0, 0)),
        ],
        out_specs=(
            pl.BlockSpec((None, s, LANES), lambda b: (b, 0, 0)),
            pl.BlockSpec((None, 4 * ML_HEADS, s), lambda b: (b, 0, 0)),
        ),
        compiler_params=_cparams("arbitrary"),
        name="ml_gates",
    )(g3, gate_b)


def _ml_scan_kernel(*refs, nb, reverse, post):
    if post:
        (q_ref, kt_ref, v_ref, gt_ref, gg_ref, c0_ref, m0_ref, hp_ref, og_ref, gain_ref,
         h_ref, c_out_ref, m_out_ref, c_sc, m_sc) = refs
    else:
        (q_ref, kt_ref, v_ref, gt_ref, gg_ref, c0_ref, m0_ref,
         h_ref, c_out_ref, m_out_ref, c_sc, m_sc) = refs
    step = pl.program_id(0)
    L = q_ref.shape[1]
    nh = ML_HEADS

    @pl.when(step == 0)
    def _():
        c_sc[...] = c0_ref[...]
        m_sc[...] = m0_ref[...]

    row = lax.broadcasted_iota(jnp.int32, (L, L), 0)
    col = lax.broadcasted_iota(jnp.int32, (L, L), 1)
    incl = (col >= row) if reverse else (col <= row)
    last = 0 if reverse else L - 1

    def lanes(h):
        ig_lane = (2 * nh if reverse else 0) + h
        return ig_lane, ig_lane + nh

    col_rep = {}
    for b in range(nb):
        gt = gt_ref[b]
        for h in range(nh):
            ig_lane, lf_lane = lanes(h)
            col_rep[b, h] = (jnp.broadcast_to(gt[:, lf_lane:lf_lane + 1], (L, LANES)),
                             jnp.broadcast_to(gt[:, ig_lane:ig_lane + 1], (L, LANES)))

    for b in range(nb):
        gg = gg_ref[b]
        for h in range(nh):
            bh = b * nh + h
            ig_lane, lf_lane = lanes(h)
            hs = slice(h * ML_DK, (h + 1) * ML_DK)
            q = q_ref[b, :, hs]
            kt = kt_ref[b, hs, :]
            v = v_ref[b, :, hs]
            c_aug = c_sc[bh]
            m_prev = m_sc[bh][0:1, :]

            bcum, ig = col_rep[b, h]
            bcum_row = gg[lf_lane:lf_lane + 1, :]
            ig_row = gg[ig_lane:ig_lane + 1, :]
            total = bcum[last:last + 1, :]

            log_d = jnp.where(incl, bcum[:, :L] - bcum_row + ig_row, -jnp.inf)
            log_prev = bcum + m_prev
            m_t = jnp.maximum(log_prev, jnp.max(log_d, axis=-1, keepdims=True))
            dmat = jnp.exp(log_d - m_t[:, :L])
            w_prev = jnp.exp(log_prev - m_t)
            s = jnp.dot(q, kt, preferred_element_type=F32) * dmat
            lhs = jnp.concatenate([s.astype(BF16), (w_prev * q.astype(F32)).astype(BF16)], axis=-1)
            rhs = jnp.concatenate([jnp.concatenate([v, jnp.ones_like(v)], axis=-1), c_aug.astype(BF16)], axis=0)
            tot = jnp.dot(lhs, rhs, preferred_element_type=F32)
            num = tot[:, :ML_DV]
            den = tot[:, ML_DV:]
            hv = num / jnp.maximum(jnp.abs(den), jnp.exp(-m_t))

            m_new = m_t[last:last + 1, :]
            w_s = jnp.exp(total - bcum + ig - m_new)
            decay = jnp.exp(total + m_prev - m_new)
            wv = jnp.concatenate([w_s * v.astype(F32), w_s], axis=-1).astype(BF16)
            kv = jnp.dot(kt, wv, preferred_element_type=F32)
            c_sc[bh] = jnp.concatenate([decay, decay], axis=-1) * c_aug + kv
            m_sc[bh] = jnp.broadcast_to(m_new, m_sc.shape[1:])

            if post:
                hv = hv + hp_ref[b, :, hs]
                hv = hv * lax.rsqrt(jnp.mean(hv * hv, axis=-1, keepdims=True) + EPS) * gain_ref[:, hs]
                og = og_ref[b, :, hs].astype(F32)
                hv = hv / (1.0 + jnp.exp(-og))
            h_ref[b, :, hs] = hv.astype(h_ref.dtype)

    @pl.when(step == pl.num_programs(0) - 1)
    def _():
        c_out_ref[...] = c_sc[...]
        m_out_ref[...] = m_sc[...]


def _ml_scan(q3, kt3, p3, gt3, gg3, state, reverse, post_args=None):
    nb, s, _ = q3.shape
    assert ML_DK == LANES and ML_DV == LANES and min(ML_CHUNK, s) <= LANES
    L = min(ML_CHUNK, s)
    nc = s // L
    w = ML_HEADS * ML_DK
    c0, m0 = state
    post = post_args is not None
    cidx = (lambda c: nc - 1 - c) if reverse else (lambda c: c)
    in_specs = [
        pl.BlockSpec((nb, L, w), lambda c: (0, cidx(c), 0)),
        pl.BlockSpec((nb, w, L), lambda c: (0, 0, cidx(c))),
        pl.BlockSpec((nb, L, w), lambda c: (0, cidx(c), C_MLV // w)),
        pl.BlockSpec((nb, L, LANES), lambda c: (0, cidx(c), 0)),
        pl.BlockSpec((nb, 4 * ML_HEADS, L), lambda c: (0, 0, cidx(c))),
        pl.BlockSpec(c0.shape, lambda c: (0, 0, 0)),
        pl.BlockSpec(m0.shape, lambda c: (0, 0, 0)),
    ]
    args = [q3, kt3, p3, gt3, gg3, c0, m0]
    if post:
        h_prev, gain = post_args
        in_specs += [
            pl.BlockSpec((nb, L, w), lambda c: (0, cidx(c), 0)),
            pl.BlockSpec((nb, L, w), lambda c: (0, cidx(c), C_MLO // w)),
            pl.BlockSpec((1, w), lambda c: (0, 0)),
        ]
        args += [h_prev, p3, gain]
    return pl.pallas_call(
        functools.partial(_ml_scan_kernel, nb=nb, reverse=reverse, post=post),
        out_shape=(jax.ShapeDtypeStruct((nb, s, w), BF16 if post else F32),
                   jax.ShapeDtypeStruct(c0.shape, F32), jax.ShapeDtypeStruct(m0.shape, F32)),
        grid=(nc,),
        in_specs=in_specs,
        out_specs=(
            pl.BlockSpec((nb, L, w), lambda c: (0, cidx(c), 0)),
            pl.BlockSpec(c0.shape, lambda c: (0, 0, 0)),
            pl.BlockSpec(m0.shape, lambda c: (0, 0, 0)),
        ),
        scratch_shapes=[pltpu.VMEM(c0.shape, F32), pltpu.VMEM(m0.shape, F32)],
        compiler_params=_cparams("arbitrary"),
        name="ml_scan",
    )(*args)


def _out_proj_kernel(da_ref, ml_ref, mla_ref, w_ref, x_ref, gate_ref, h_ref):
    n_da = da_ref.shape[1]
    n_ml = ml_ref.shape[1]
    acc = jnp.dot(da_ref[...], w_ref[0:n_da, :].astype(BF16), preferred_element_type=F32)
    acc += jnp.dot(ml_ref[...], w_ref[n_da:n_da + n_ml, :].astype(BF16), preferred_element_type=F32)
    acc += jnp.dot(mla_ref[...], w_ref[n_da + n_ml:, :].astype(BF16), preferred_element_type=F32)
    h_ref[...] = x_ref[...] + gate_ref[...] * acc


def _out_proj(da, ml, mla, w, layer, x, mod, rows_per_mod, mod_base):
    r, d = x.shape
    tm = _pick(min(r, rows_per_mod), 1024)
    tn = _pick(d, 1024)
    nt = d // tn
    return pl.pallas_call(
        _out_proj_kernel,
        out_shape=jax.ShapeDtypeStruct((r, d), F32),
        grid=(nt, r // tm),
        in_specs=[
            pl.BlockSpec((tm, da.shape[1]), lambda j, i: (i, 0)),
            pl.BlockSpec((tm, ml.shape[1]), lambda j, i: (i, 0)),
            pl.BlockSpec((tm, mla.shape[1]), lambda j, i: (i, 0)),
            pl.BlockSpec((None, w.shape[1], tn), lambda j, i: (layer, 0, j)),
            pl.BlockSpec((tm, tn), lambda j, i: (i, j)),
            pl.BlockSpec((None, 1, tn), lambda j, i: (i * tm // rows_per_mod + mod_base, 0, 2 * nt + j)),
        ],
        out_specs=pl.BlockSpec((tm, tn), lambda j, i: (i, j)),
        compiler_params=_cparams("arbitrary", "arbitrary"),
        name="out_proj",
    )(da, ml, mla, w, x, mod)


def _ffn_kernel(h_ref, mod_ref, g_ref, wg_ref, wu_ref, wd_ref, o_ref, hn_ref, *, d, rc):
    j = pl.program_id(1)

    @pl.when(j == 0)
    def _():
        shift = mod_ref[:, 3 * d:4 * d]
        scale1 = 1.0 + mod_ref[:, 4 * d:5 * d]
        _norm_rows(h_ref, g_ref, shift, scale1, hn_ref, rc)
        o_ref[...] = h_ref[...]

    hn = hn_ref[...]
    gate = jnp.dot(hn, wg_ref[...], preferred_element_type=F32)
    up = jnp.dot(hn, wu_ref[...], preferred_element_type=F32)
    act = (gate / (1.0 + jnp.exp(-gate)) * up).astype(BF16)
    o_ref[...] += mod_ref[:, 5 * d:6 * d] * jnp.dot(act, wd_ref[...], preferred_element_type=F32)


def _ffn(h, mod, g, w_gu, w_down, layer, rows_per_mod, mod_base):
    r, d = h.shape
    ff = w_down.shape[1]
    tm = _pick(min(r, rows_per_mod), 1024)
    th = 512 if ff % 512 == 0 else 256
    nh = ff // th
    return pl.pallas_call(
        functools.partial(_ffn_kernel, d=d, rc=_pick(tm, 64)),
        out_shape=jax.ShapeDtypeStruct((r, d), F32),
        grid=(r // tm, nh),
        in_specs=[
            pl.BlockSpec((tm, d), lambda i, j: (i, 0)),
            pl.BlockSpec((None, 1, 6 * d), lambda i, j: (i * tm // rows_per_mod + mod_base, 0, 0)),
            pl.BlockSpec((1, d), lambda i, j: (0, 0)),
            pl.BlockSpec((None, d, th), lambda i, j: (layer, 0, j)),
            pl.BlockSpec((None, d, th), lambda i, j: (layer, 0, nh + j)),
            pl.BlockSpec((None, th, d), lambda i, j: (layer, j, 0)),
        ],
        out_specs=pl.BlockSpec((tm, d), lambda i, j: (i, 0)),
        scratch_shapes=[pltpu.VMEM((tm, d), BF16)],
        compiler_params=_cparams("arbitrary", "arbitrary"),
        name="ffn",
    )(h, mod, g, w_gu, w_gu, w_down)


def _rope_tables(n_lat):
    pos = jnp.arange(n_lat, dtype=jnp.int32)
    r = (pos // GRID_W).astype(F32)
    c = (pos % GRID_W).astype(F32)
    half = ROPE_DIM // 4
    inv = ROPE_BASE ** (-jnp.arange(half, dtype=F32) / half)
    ar = r[:, None] * inv
    ac = c[:, None] * inv
    cos64 = jnp.concatenate([jnp.cos(ar), jnp.cos(ar), jnp.cos(ac), jnp.cos(ac)], axis=-1)
    sin64 = jnp.concatenate([-jnp.sin(ar), jnp.sin(ar), -jnp.sin(ac), jnp.sin(ac)], axis=-1)
    return cos64, sin64


def _pad_cols(a, width):
    return jnp.pad(a, ((0, 0),) * (a.ndim - 1) + ((0, width - a.shape[-1]),))


def _layout_w_in(w):
    gates0 = C_CQ
    gates1 = gates0 + 4 * ML_HEADS
    main = w[..., :gates0]
    gates = w[..., gates0:gates1]
    rest = w[..., gates1:]
    n_cq_ckv = C_KPE - C_CQ
    cq_ckv = rest[..., :n_cq_ckv]
    kpe = rest[..., n_cq_ckv:]
    return jnp.concatenate([main, cq_ckv, _pad_cols(kpe, LANES), _pad_cols(gates, LANES)], axis=-1)


def kernel(x, c, ctx, c_ctx, mod_w, mod_b, norm1_g, norm2_g, w_in, da_qk_g, da_lambda, da_out_g,
           ml_conv_w, ml_conv_b, ml_gate_b, ml_out_g, mla_q_norm_g, mla_kv_norm_g, mla_w_uq, mla_w_ukv,
           mla_qk_g, w_out, ffn_w_gu, ffn_w_down):
    batch, seq, d = x.shape
    ctx_len = ctx.shape[1]
    depth = mod_w.shape[0]
    in_dtype = x.dtype
    assert w_in.shape[2] == C_CQ + 4 * ML_HEADS + (C_KPE - C_CQ) + MLA_ROPE

    n_mod_rows = -(-(batch + 1) // 8) * 8
    cc = jnp.concatenate([c, c_ctx[None, :], jnp.zeros((n_mod_rows - batch - 1, d), F32)], axis=0)
    mod_all = _modulation(cc, mod_w, mod_b)

    cos64, sin64 = _rope_tables(seq)
    cos_da = jnp.tile(cos64, (1, 2))
    sin_da = jnp.tile(sin64, (1, 2))
    cos_mla = jnp.concatenate([cos64, jnp.ones_like(cos64)], axis=-1)
    sin_mla = jnp.concatenate([sin64, jnp.zeros_like(sin64)], axis=-1)
    tab_rows = _pick(batch * ctx_len, 1024)
    cos_id = jnp.ones((tab_rows, LANES), F32)
    sin_id = jnp.zeros((tab_rows, LANES), F32)
    lane = jnp.arange(LANES)
    group_mean = jnp.where((lane[:, None] // DA_HALF) == (lane[None, :] // DA_HALF),
                           1.0 / DA_HALF, 0.0).astype(BF16)

    w_in_b = _layout_w_in(w_in.astype(BF16))
    w_gu_b = ffn_w_gu.astype(BF16)
    w_down_b = ffn_w_down.astype(BF16)
    wq_b = jnp.pad(mla_w_uq.reshape(depth, -1, MLA_HEADS, MLA_QK),
                   ((0, 0), (0, 0), (0, 0), (0, MLA_PAD - MLA_QK))).reshape(depth, -1, MLA_HEADS * MLA_PAD).astype(BF16)
    wkv_b = mla_w_ukv.astype(BF16)

    xl = x.reshape(batch * seq, d)
    xc = ctx.reshape(batch * ctx_len, d)
    huge = batch * max(seq, ctx_len) * 2

    for l in range(depth):
        need_ctx = l < depth - 1
        lam_init = 0.8 - 0.6 * math.exp(-0.3 * l)
        mod = mod_all[l].reshape(n_mod_rows, 1, 6 * d)
        g1 = norm1_g[l].reshape(1, d)
        g2 = norm2_g[l].reshape(1, d)
        gq_da = jnp.tile(da_qk_g[l, 0], 2).reshape(1, LANES)
        gk_da = jnp.tile(da_qk_g[l, 1], 2).reshape(1, LANES)
        gq_mla = _pad_cols(mla_qk_g[l, 0].reshape(1, MLA_QK), MLA_PAD)
        gk_mla = _pad_cols(mla_qk_g[l, 1].reshape(1, MLA_QK), MLA_PAD)
        gcq = mla_q_norm_g[l].reshape(1, -1)
        gckv = mla_kv_norm_g[l].reshape(1, -1)
        gate_b = _pad_cols(ml_gate_b[l].reshape(1, -1), LANES)
        conv_w = ml_conv_w[l]
        conv_b = ml_conv_b[l].reshape(1, -1)
        ml_gain = ml_out_g[l].reshape(1, -1)
        da_gain = da_out_g[l].reshape(1, -1)
        lam_p = da_lambda[l]

        p_l, g_l = _in_proj(xl, mod, g1, w_in_b, l, seq, 0)
        p_c, g_c = _in_proj(xc, mod, g1, w_in_b, l, huge, batch)

        qd_l, kd_l = _da_prep(p_l, cos_da, sin_da, gq_da, gk_da, group_mean)
        qd_c, kd_c = _da_prep(p_c, cos_id, sin_id, gq_da, gk_da, group_mean)
        da_l = _attention("da", qd_l, [(kd_l, p_l, C_DAV), (kd_c, p_c, C_DAV)], batch,
                          [lam_p, da_gain], lam_init)
        qm_l, km_l, vm_l = _mla_prep(p_l, cos_mla, sin_mla, gcq, gckv, gq_mla, gk_mla, wq_b, wkv_b, l)
        qm_c, km_c, vm_c = _mla_prep(p_c, cos_id, sin_id, gcq, gckv, gq_mla, gk_mla, wq_b, wkv_b, l)
        mla_l = _attention("mla", qm_l, [(km_l, vm_l, 0), (km_c, vm_c, 0)], batch, [])

        p3_l = p_l.reshape(batch, seq, P_COLS)
        p3_c = p_c.reshape(batch, ctx_len, P_COLS)
        q_l, kt_l = _ml_conv(p3_l, conv_w, conv_b)
        q_c, kt_c = _ml_conv(p3_c, conv_w, conv_b)
        gt_l, gg_l = _ml_gates(g_l.reshape(batch, seq, LANES), gate_b)
        gt_c, gg_c = _ml_gates(g_c.reshape(batch, ctx_len, LANES), gate_b)
        s0 = (jnp.zeros((batch * ML_HEADS, ML_DK, 2 * ML_DV), F32),
              jnp.zeros((batch * ML_HEADS, 8, LANES), F32))
        h_cf, c_f, m_f = _ml_scan(q_c, kt_c, p3_c, gt_c, gg_c, s0, False)
        h_lf, _, _ = _ml_scan(q_l, kt_l, p3_l, gt_l, gg_l, (c_f, m_f), False)
        if need_ctx:
            ml_c, c_b, m_b = _ml_scan(q_c, kt_c, p3_c, gt_c, gg_c, s0, True, (h_cf, ml_gain))
        else:
            _, c_b, m_b = _ml_scan(q_c, kt_c, p3_c, gt_c, gg_c, s0, True)
        ml_l, _, _ = _ml_scan(q_l, kt_l, p3_l, gt_l, gg_l, (c_b, m_b), True, (h_lf, ml_gain))

        h_l = _out_proj(da_l, ml_l.reshape(batch * seq, -1), mla_l, w_out, l, xl, mod, seq, 0)
        xl = _ffn(h_l, mod, g2, w_gu_b, w_down_b, l, seq, 0)
        if need_ctx:
            da_c = _attention("da", qd_c, [(kd_c, p_c, C_DAV)], batch, [lam_p, da_gain], lam_init)
            mla_c = _attention("mla", qm_c, [(km_c, vm_c, 0)], batch, [])
            h_c = _out_proj(da_c, ml_c.reshape(batch * ctx_len, -1), mla_c, w_out, l, xc, mod, huge, batch)
            xc = _ffn(h_c, mod, g2, w_gu_b, w_down_b, l, huge, batch)

    return xl.reshape(batch, seq, d).astype(in_dtype)
```

```python
import functools
import math

import jax
import jax.numpy as jnp
from jax import lax
from jax.experimental import pallas as pl
from jax.experimental.pallas import tpu as pltpu

F32 = jnp.float32
BF16 = jnp.bfloat16

DA_HEADS = 4
DA_HALF = 64
DA_V = 2 * DA_HALF
ML_HEADS = 4
ML_DK = 128
ML_DV = 128
MLA_HEADS = 8
MLA_NOPE = 128
MLA_ROPE = 64
MLA_V = 128
MLA_QK = MLA_NOPE + MLA_ROPE
MLA_PAD = 256
GRID_W = 64
ROPE_DIM = 64
ROPE_BASE = 10000.0
EPS = 1e-6
LOG2E = 1.4426950408889634

LANES = 128
ML_CHUNK = 128
VMEM_LIMIT = 56 * 1024 * 1024

C_DAQ, C_DAK, C_DAV = 0, 512, 1024
C_MLQ, C_MLK, C_MLV, C_MLO = 1536, 2048, 2560, 3072
C_CQ, C_CKV, C_KPE, C_GATE = 3584, 4096, 4352, 4480
P_COLS = 4608


def _cparams(*sem):
    return pltpu.CompilerParams(dimension_semantics=sem, vmem_limit_bytes=VMEM_LIMIT)


def _pick(n, pref):
    if n <= pref:
        return n
    t = pref
    while n % t:
        t //= 2
    return t


def _mod_kernel(c_ref, w_ref, b_ref, o_ref):
    c = c_ref[...]
    s = (c / (1.0 + jnp.exp(-c))).astype(BF16)
    o_ref[...] = jnp.dot(s, w_ref[...].astype(BF16), preferred_element_type=F32) + b_ref[...]


def _modulation(cc, mod_w, mod_b):
    depth, d, n = mod_w.shape
    tn = _pick(n, 1024)
    return pl.pallas_call(
        _mod_kernel,
        out_shape=jax.ShapeDtypeStruct((depth, cc.shape[0], n), F32),
        grid=(depth, n // tn),
        in_specs=[
            pl.BlockSpec((cc.shape[0], d), lambda l, j: (0, 0)),
            pl.BlockSpec((None, d, tn), lambda l, j: (l, 0, j)),
            pl.BlockSpec((None, 1, tn), lambda l, j: (l, 0, j)),
        ],
        out_specs=pl.BlockSpec((None, cc.shape[0], tn), lambda l, j: (l, 0, j)),
        compiler_params=_cparams("arbitrary", "arbitrary"),
        name="modulation",
    )(cc, mod_w, mod_b.reshape(depth, 1, n))


def _norm_rows(x_ref, g_ref, shift, scale1, out_ref, rc):
    tm = x_ref.shape[0]
    gs = g_ref[...] * scale1

    def body(r, carry):
        rows = pl.ds(pl.multiple_of(r * rc, rc), rc)
        x = x_ref[rows, :]
        ms = jnp.mean(x * x, axis=-1, keepdims=True)
        out_ref[rows, :] = (x * lax.rsqrt(ms + EPS) * gs + shift).astype(out_ref.dtype)
        return carry

    lax.fori_loop(0, tm // rc, body, 0, unroll=2)


def _in_proj_kernel(x_ref, mod_ref, g_ref, w_ref, p_ref, gate_ref, xn_ref, *, d, rc):
    n = pl.program_id(1)

    @pl.when(n == 0)
    def _():
        shift = mod_ref[:, 0:d]
        scale1 = 1.0 + mod_ref[:, d:2 * d]
        _norm_rows(x_ref, g_ref, shift, scale1, xn_ref, rc)

    acc = jnp.dot(xn_ref[...], w_ref[...], preferred_element_type=F32)
    p_ref[...] = acc.astype(p_ref.dtype)

    @pl.when(n == pl.num_programs(1) - 1)
    def _():
        gate_ref[...] = acc[:, acc.shape[1] - LANES:]


def _in_proj(x, mod, g, w, layer, rows_per_mod, mod_base):
    r, d = x.shape
    n = w.shape[2]
    tm = _pick(min(r, rows_per_mod), 1024)
    tn = 1536
    assert n % tn == 0
    return pl.pallas_call(
        functools.partial(_in_proj_kernel, d=d, rc=_pick(tm, 64)),
        out_shape=(jax.ShapeDtypeStruct((r, n), BF16), jax.ShapeDtypeStruct((r, LANES), F32)),
        grid=(r // tm, n // tn),
        in_specs=[
            pl.BlockSpec((tm, d), lambda i, j: (i, 0)),
            pl.BlockSpec((None, 1, 6 * d), lambda i, j: (i * tm // rows_per_mod + mod_base, 0, 0)),
            pl.BlockSpec((1, d), lambda i, j: (0, 0)),
            pl.BlockSpec((None, d, tn), lambda i, j: (layer, 0, j)),
        ],
        out_specs=(
            pl.BlockSpec((tm, tn), lambda i, j: (i, j)),
            pl.BlockSpec((tm, LANES), lambda i, j: (i, 0)),
        ),
        scratch_shapes=[pltpu.VMEM((tm, d), BF16)],
        compiler_params=_cparams("arbitrary", "arbitrary"),
        name="in_proj",
    )(x, mod, g, w)


def _swap16(x):
    lane = lax.broadcasted_iota(jnp.int32, x.shape, 1)
    fwd = pltpu.roll(x, LANES - 16, 1)
    bwd = pltpu.roll(x, 16, 1)
    return jnp.where((lane % 32) < 16, fwd, bwd)


def _rope(x, cos, sin):
    return x * cos + _swap16(x) * sin


def _split3(x):
    hi = x.astype(BF16)
    r1 = x - hi.astype(F32)
    mid = r1.astype(BF16)
    lo = (r1 - mid.astype(F32)).astype(BF16)
    return hi, mid, lo


def _group_mean_sq(x, gm):
    x2 = x * x
    hi = x2.astype(BF16)
    lo = (x2 - hi.astype(F32)).astype(BF16)
    return jnp.dot(hi, gm, preferred_element_type=F32) + jnp.dot(lo, gm, preferred_element_type=F32)


def _da_prep_kernel(q_ref, k_ref, cos_ref, sin_ref, gq_ref, gk_ref, gm_ref, qo_ref, ko_ref, *, qscale):
    cos = cos_ref[...]
    sin = sin_ref[...]
    gm = gm_ref[...]
    lane = lax.broadcasted_iota(jnp.int32, cos.shape, 1)
    first = lane < DA_HALF
    for h in range(DA_HEADS):
        sl = slice(h * LANES, (h + 1) * LANES)
        q = q_ref[:, sl].astype(F32)
        q = q * lax.rsqrt(_group_mean_sq(q, gm) + EPS) * gq_ref[...]
        q = _rope(q, cos, sin) * qscale
        qo_ref[:, 2 * h * LANES:(2 * h + 1) * LANES] = jnp.where(first, q, 0.0).astype(qo_ref.dtype)
        qo_ref[:, (2 * h + 1) * LANES:(2 * h + 2) * LANES] = jnp.where(first, 0.0, q).astype(qo_ref.dtype)
        k = k_ref[:, sl].astype(F32)
        k = k * lax.rsqrt(_group_mean_sq(k, gm) + EPS) * gk_ref[...]
        ko_ref[:, sl] = _rope(k, cos, sin).astype(ko_ref.dtype)


def _da_prep(p, cos, sin, gq, gk, gm):
    r = p.shape[0]
    tm = _pick(min(r, cos.shape[0]), 1024)
    tab_blocks = cos.shape[0] // tm
    w = DA_HEADS * LANES
    qscale = DA_HALF ** -0.5 * LOG2E
    return pl.pallas_call(
        functools.partial(_da_prep_kernel, qscale=qscale),
        out_shape=(jax.ShapeDtypeStruct((r, 2 * w), BF16), jax.ShapeDtypeStruct((r, w), BF16)),
        grid=(r // tm,),
        in_specs=[
            pl.BlockSpec((tm, w), lambda i: (i, C_DAQ // w)),
            pl.BlockSpec((tm, w), lambda i: (i, C_DAK // w)),
            pl.BlockSpec((tm, LANES), lambda i: (i % tab_blocks, 0)),
            pl.BlockSpec((tm, LANES), lambda i: (i % tab_blocks, 0)),
            pl.BlockSpec((1, LANES), lambda i: (0, 0)),
            pl.BlockSpec((1, LANES), lambda i: (0, 0)),
            pl.BlockSpec((LANES, LANES), lambda i: (0, 0)),
        ],
        out_specs=(
            pl.BlockSpec((tm, 2 * w), lambda i: (i, 0)),
            pl.BlockSpec((tm, w), lambda i: (i, 0)),
        ),
        compiler_params=_cparams("arbitrary"),
        name="da_prep",
    )(p, p, cos, sin, gq, gk, gm)


def _mla_prep_kernel(cq_ref, ckv_ref, kpe_ref, cos_ref, sin_ref, gcq_ref, gckv_ref, gq_ref, gk_ref,
                     wq_ref, wkv_ref, qo_ref, ko_ref, vo_ref, *, qscale):
    cos = cos_ref[...]
    sin = sin_ref[...]
    lane = lax.broadcasted_iota(jnp.int32, cos.shape, 1)
    inv_qk = 1.0 / MLA_QK

    cq = cq_ref[...].astype(F32)
    cq = cq * lax.rsqrt(jnp.mean(cq * cq, axis=-1, keepdims=True) + EPS) * gcq_ref[...]
    qf = jnp.dot(cq.astype(BF16), wq_ref[...], preferred_element_type=F32)

    ckv = ckv_ref[...].astype(F32)
    ckv = ckv * lax.rsqrt(jnp.mean(ckv * ckv, axis=-1, keepdims=True) + EPS) * gckv_ref[...]
    kvf = jnp.dot(ckv.astype(BF16), wkv_ref[...], preferred_element_type=F32)

    kpe = jnp.where(lane < MLA_ROPE, kpe_ref[...].astype(F32), 0.0)
    kpe_sq = kpe * kpe
    kpe_rot = _rope(kpe * gk_ref[:, LANES:], cos, sin)

    for h in range(MLA_HEADS):
        q1 = qf[:, h * MLA_PAD:h * MLA_PAD + LANES]
        q2 = qf[:, h * MLA_PAD + LANES:(h + 1) * MLA_PAD]
        ms = jnp.sum(q1 * q1 + q2 * q2, axis=-1, keepdims=True) * inv_qk
        rq = lax.rsqrt(ms + EPS) * qscale
        qo_ref[:, h * MLA_PAD:h * MLA_PAD + LANES] = (q1 * rq * gq_ref[:, :LANES]).astype(qo_ref.dtype)
        q2 = _rope(q2 * rq * gq_ref[:, LANES:], cos, sin)
        qo_ref[:, h * MLA_PAD + LANES:(h + 1) * MLA_PAD] = q2.astype(qo_ref.dtype)

        kn = kvf[:, h * MLA_PAD:h * MLA_PAD + LANES]
        ms = jnp.sum(kn * kn + kpe_sq, axis=-1, keepdims=True) * inv_qk
        rk = lax.rsqrt(ms + EPS)
        ko_ref[:, h * MLA_PAD:h * MLA_PAD + LANES] = (kn * rk * gk_ref[:, :LANES]).astype(ko_ref.dtype)
        ko_ref[:, h * MLA_PAD + LANES:(h + 1) * MLA_PAD] = (kpe_rot * rk).astype(ko_ref.dtype)
        vo_ref[:, h * MLA_V:(h + 1) * MLA_V] = kvf[:, h * MLA_PAD + LANES:(h + 1) * MLA_PAD].astype(vo_ref.dtype)


def _mla_prep(p, cos, sin, gcq, gckv, gq, gk, wq, wkv, layer):
    r = p.shape[0]
    tm = _pick(min(r, cos.shape[0]), 1024)
    tab_blocks = cos.shape[0] // tm
    q_rank = wq.shape[1]
    kv_rank = wkv.shape[1]
    wide = MLA_HEADS * MLA_PAD
    qscale = MLA_QK ** -0.5 * LOG2E
    const = lambda i: (0, 0)
    return pl.pallas_call(
        functools.partial(_mla_prep_kernel, qscale=qscale),
        out_shape=(jax.ShapeDtypeStruct((r, wide), BF16), jax.ShapeDtypeStruct((r, wide), BF16),
                   jax.ShapeDtypeStruct((r, MLA_HEADS * MLA_V), BF16)),
        grid=(r // tm,),
        in_specs=[
            pl.BlockSpec((tm, q_rank), lambda i: (i, C_CQ // q_rank)),
            pl.BlockSpec((tm, kv_rank), lambda i: (i, C_CKV // kv_rank)),
            pl.BlockSpec((tm, LANES), lambda i: (i, C_KPE // LANES)),
            pl.BlockSpec((tm, LANES), lambda i: (i % tab_blocks, 0)),
            pl.BlockSpec((tm, LANES), lambda i: (i % tab_blocks, 0)),
            pl.BlockSpec((1, q_rank), const),
            pl.BlockSpec((1, kv_rank), const),
            pl.BlockSpec((1, MLA_PAD), const),
            pl.BlockSpec((1, MLA_PAD), const),
            pl.BlockSpec((None, q_rank, wide), lambda i: (layer, 0, 0)),
            pl.BlockSpec((None, kv_rank, wide), lambda i: (layer, 0, 0)),
        ],
        out_specs=(
            pl.BlockSpec((tm, wide), lambda i: (i, 0)),
            pl.BlockSpec((tm, wide), lambda i: (i, 0)),
            pl.BlockSpec((tm, MLA_HEADS * MLA_V), lambda i: (i, 0)),
        ),
        compiler_params=_cparams("arbitrary"),
        name="mla_prep",
    )(p, p, p, cos, sin, gcq, gckv, gq, gk, wq, wkv)


def _attn_core(q_ref, k_refs, v_refs, n_comp, dq, kc):
    chunks = []
    for ri, kr in enumerate(k_refs):
        rows = kr.shape[0]
        step = min(kc, rows)
        for r0 in range(0, rows, step):
            chunks.append((ri, r0, step))
    outs = []
    for c in range(n_comp):
        q = q_ref[:, c * dq:(c + 1) * dq]
        m = acc = None
        for ri, r0, step in chunks:
            s = lax.dot_general(q, k_refs[ri][r0:r0 + step, :], (((1,), (1,)), ((), ())),
                                preferred_element_type=F32)
            cm = jnp.max(s, axis=-1, keepdims=True)
            m_new = cm if m is None else jnp.maximum(m, cm)
            p = jnp.exp2(s - m_new)
            vc = v_refs[ri][r0:r0 + step, :]
            pv = jnp.dot(p.astype(BF16), jnp.concatenate([vc, jnp.ones_like(vc)], axis=-1),
                         preferred_element_type=F32)
            acc = pv if m is None else jnp.exp2(m - m_new) * acc + pv
            m = m_new
        dv = acc.shape[1] // 2
        outs.append(acc[:, :dv] / acc[:, dv:])
    return outs


def _da_attn_kernel(*refs, n_kv, lam_init, kc):
    q_ref = refs[0]
    k_refs = refs[1:1 + n_kv]
    v_refs = refs[1 + n_kv:1 + 2 * n_kv]
    lam_ref, g_ref, o_ref = refs[1 + 2 * n_kv:]
    lp = lam_ref[...]
    lam = (jnp.exp(jnp.sum(lp[0:1] * lp[1:2], axis=-1, keepdims=True))
           - jnp.exp(jnp.sum(lp[2:3] * lp[3:4], axis=-1, keepdims=True)) + lam_init)
    o0, o1 = _attn_core(q_ref, k_refs, v_refs, 2, LANES, kc)
    o = o0 - lam * o1
    o = o * lax.rsqrt(jnp.mean(o * o, axis=-1, keepdims=True) + EPS) * g_ref[...]
    o_ref[...] = (o * (1.0 - lam_init)).astype(o_ref.dtype)


def _mla_attn_kernel(*refs, n_kv, kc):
    q_ref = refs[0]
    k_refs = refs[1:1 + n_kv]
    v_refs = refs[1 + n_kv:1 + 2 * n_kv]
    (o_ref,) = refs[1 + 2 * n_kv:]
    (o,) = _attn_core(q_ref, k_refs, v_refs, 1, MLA_PAD, kc)
    o_ref[...] = o.astype(o_ref.dtype)


def _attention(kind, q, kvs, batch, extra, lam_init=0.0):
    heads = DA_HEADS if kind == "da" else MLA_HEADS
    qw = 2 * LANES if kind == "da" else MLA_PAD
    dq = LANES if kind == "da" else MLA_PAD
    dv = DA_V if kind == "da" else MLA_V
    rq = q.shape[0]
    tq = _pick(rq // batch, 2048)
    nq = rq // batch // tq
    kc = 256
    k_specs, v_specs, k_args, v_args = [], [], [], []
    for k, v, vcol in kvs:
        kl = k.shape[0] // batch
        k_specs.append(pl.BlockSpec((kl, dq), lambda b, h, i: (b, h)))
        v_specs.append(pl.BlockSpec((kl, dv), lambda b, h, i, vcol=vcol: (b, vcol // dv + h)))
        k_args.append(k)
        v_args.append(v)
    extra_specs = [pl.BlockSpec(e.shape, lambda b, h, i: (0, 0)) for e in extra]
    if kind == "da":
        body = functools.partial(_da_attn_kernel, n_kv=len(kvs), lam_init=lam_init, kc=kc)
    else:
        body = functools.partial(_mla_attn_kernel, n_kv=len(kvs), kc=kc)
    return pl.pallas_call(
        body,
        out_shape=jax.ShapeDtypeStruct((rq, heads * dv), BF16),
        grid=(batch, heads, nq),
        in_specs=[pl.BlockSpec((tq, qw), lambda b, h, i: (b * nq + i, h))] + k_specs + v_specs + extra_specs,
        out_specs=pl.BlockSpec((tq, dv), lambda b, h, i: (b * nq + i, h)),
        compiler_params=_cparams("arbitrary", "arbitrary", "arbitrary"),
        name=kind + "_attn",
    )(q, *k_args, *v_args, *extra)


def _ml_conv_kernel(xq_ref, xk_ref, wq_ref, wk_ref, bq_ref, bk_ref, q_ref, kt_ref, pad_ref, *, kscale, tc):
    s = xq_ref.shape[0]
    zeros = jnp.zeros((8, LANES), F32)
    pad_ref[0:8, :] = zeros
    pad_ref[s + 8:s + 16, :] = zeros

    def conv_silu(x_ref, w_ref, b_ref):
        x = x_ref[...].astype(F32)
        pad_ref[8:s + 8, :] = x
        prev = pad_ref[7:s + 7, :]
        nxt = pad_ref[9:s + 9, :]
        y = b_ref[...] + prev * w_ref[0:1, :] + x * w_ref[1:2, :] + nxt * w_ref[2:3, :]
        return y / (1.0 + jnp.exp(-y))

    q_ref[...] = conv_silu(xq_ref, wq_ref, bq_ref).astype(q_ref.dtype)
    k = conv_silu(xk_ref, wk_ref, bk_ref) * kscale
    for r0 in range(0, s, tc):
        kt_ref[:, r0:r0 + tc] = k[r0:r0 + tc, :].T.astype(kt_ref.dtype)


def _ml_conv(p3, conv_w, conv_b):
    nseq, s, _ = p3.shape
    w = ML_HEADS * ML_DK
    q0 = C_MLQ // LANES
    k0 = C_MLK // LANES
    return pl.pallas_call(
        functools.partial(_ml_conv_kernel, kscale=ML_DK ** -0.5, tc=_pick(s, 512)),
        out_shape=(jax.ShapeDtypeStruct((nseq, s, w), BF16), jax.ShapeDtypeStruct((nseq, w, s), BF16)),
        grid=(nseq, ML_HEADS),
        in_specs=[
            pl.BlockSpec((None, s, LANES), lambda b, h: (b, 0, q0 + h)),
            pl.BlockSpec((None, s, LANES), lambda b, h: (b, 0, k0 + h)),
            pl.BlockSpec((3, LANES), lambda b, h: (0, h)),
            pl.BlockSpec((3, LANES), lambda b, h: (0, ML_HEADS + h)),
            pl.BlockSpec((1, LANES), lambda b, h: (0, h)),
            pl.BlockSpec((1, LANES), lambda b, h: (0, ML_HEADS + h)),
        ],
        out_specs=(
            pl.BlockSpec((None, s, LANES), lambda b, h: (b, 0, h)),
            pl.BlockSpec((None, ML_DK, s), lambda b, h: (b, h, 0)),
        ),
        scratch_shapes=[pltpu.VMEM((s + 16, LANES), F32)],
        compiler_params=_cparams("arbitrary", "arbitrary"),
        name="ml_conv",
    )(p3, p3, conv_w, conv_w, conv_b, conv_b)


def _ml_gate_kernel(g_ref, b_ref, gt_ref, gg_ref, *, chunk):
    s = g_ref.shape[0]
    row = lax.broadcasted_iota(jnp.int32, (chunk, chunk), 0)
    col = lax.broadcasted_iota(jnp.int32, (chunk, chunk), 1)
    prefix = jnp.where(col <= row, 1.0, 0.0).astype(BF16)
    suffix = jnp.where(col >= row, 1.0, 0.0).astype(BF16)
    lane = lax.broadcasted_iota(jnp.int32, (chunk, LANES), 1)
    is_forget = ((lane % (2 * ML_HEADS)) >= ML_HEADS) & (lane < 4 * ML_HEADS)
    is_fwd = lane < 2 * ML_HEADS
    for r0 in range(0, s, chunk):
        g = g_ref[r0:r0 + chunk, :] + b_ref[...]
        lf = jnp.minimum(g, 0.0) - jnp.log(1.0 + jnp.exp(-jnp.abs(g)))
        parts = _split3(jnp.where(is_forget, lf, 0.0))
        cum_f = sum(jnp.dot(prefix, part, preferred_element_type=F32) for part in parts)
        cum_b = sum(jnp.dot(suffix, part, preferred_element_type=F32) for part in parts)
        out = jnp.where(is_forget, jnp.where(is_fwd, cum_f, cum_b), g)
        gt_ref[r0:r0 + chunk, :] = out
        gg_ref[:, r0:r0 + chunk] = out.T[0:4 * ML_HEADS, :]


def _ml_gates(g3, gate_b):
    nseq, s, _ = g3.shape
    return pl.pallas_call(
        functools.partial(_ml_gate_kernel, chunk=min(ML_CHUNK, s)),
        out_shape=(jax.ShapeDtypeStruct((nseq, s, LANES), F32),
                   jax.ShapeDtypeStruct((nseq, 4 * ML_HEADS, s), F32)),
        grid=(nseq,),
        in_specs=[
            pl.BlockSpec((None, s, LANES), lambda b: (b, 0, 0)),
            pl.BlockSpec((1, LANES), lambda b: (0, 0)),
        ],
        out_specs=(
            pl.BlockSpec((None, s, LANES), lambda b: (b, 0, 0)),
            pl.BlockSpec((None, 4 * ML_HEADS, s), lambda b: (b, 0, 0)),
        ),
        compiler_params=_cparams("arbitrary"),
        name="ml_gates",
    )(g3, gate_b)


def _ml_scan_kernel(*refs, nb, reverse, post, sub):
    if post:
        (q_ref, kt_ref, v_ref, gt_ref, gg_ref, c0_ref, m0_ref, hp_ref, og_ref, gain_ref,
         h_ref, c_out_ref, m_out_ref, c_sc, m_sc) = refs
    else:
        (q_ref, kt_ref, v_ref, gt_ref, gg_ref, c0_ref, m0_ref,
         h_ref, c_out_ref, m_out_ref, c_sc, m_sc) = refs
    step = pl.program_id(0)
    L = q_ref.shape[1] // sub
    nh = ML_HEADS

    @pl.when(step == 0)
    def _():
        c_sc[...] = c0_ref[...]
        m_sc[...] = m0_ref[...]

    row = lax.broadcasted_iota(jnp.int32, (L, L), 0)
    col = lax.broadcasted_iota(jnp.int32, (L, L), 1)
    incl = (col >= row) if reverse else (col <= row)
    last = 0 if reverse else L - 1

    def lanes(h):
        ig_lane = (2 * nh if reverse else 0) + h
        return ig_lane, ig_lane + nh

    order = list(reversed(range(sub))) if reverse else list(range(sub))
    col_rep = {}
    for sc in order:
        for b in range(nb):
            gt = gt_ref[b, sc * L:(sc + 1) * L, :]
            for h in range(nh):
                ig_lane, lf_lane = lanes(h)
                col_rep[sc, b, h] = (jnp.broadcast_to(gt[:, lf_lane:lf_lane + 1], (L, LANES)),
                                     jnp.broadcast_to(gt[:, ig_lane:ig_lane + 1], (L, LANES)))

    for sc in order:
        r0 = sc * L
        for b in range(nb):
            gg = gg_ref[b][:, r0:r0 + L]
            for h in range(nh):
                bh = b * nh + h
                ig_lane, lf_lane = lanes(h)
                hs = slice(h * ML_DK, (h + 1) * ML_DK)
                q = q_ref[b, r0:r0 + L, hs]
                kt = kt_ref[b, hs, r0:r0 + L]
                v = v_ref[b, r0:r0 + L, hs]
                c_aug = c_sc[bh]
                m_prev = m_sc[bh][0:1, :]

                bcum, ig = col_rep[sc, b, h]
                bcum_row = gg[lf_lane:lf_lane + 1, :]
                ig_row = gg[ig_lane:ig_lane + 1, :]
                total = bcum[last:last + 1, :]

                log_d = jnp.where(incl, bcum[:, :L] - bcum_row + ig_row, -jnp.inf)
                log_prev = bcum + m_prev
                m_t = jnp.maximum(log_prev, jnp.max(log_d, axis=-1, keepdims=True))
                dmat = jnp.exp(log_d - m_t[:, :L])
                w_prev = jnp.exp(log_prev - m_t)
                s = jnp.dot(q, kt, preferred_element_type=F32) * dmat
                lhs = jnp.concatenate([s.astype(BF16), (w_prev * q.astype(F32)).astype(BF16)], axis=-1)
                rhs = jnp.concatenate([jnp.concatenate([v, jnp.ones_like(v)], axis=-1), c_aug.astype(BF16)], axis=0)
                tot = jnp.dot(lhs, rhs, preferred_element_type=F32)
                num = tot[:, :ML_DV]
                den = tot[:, ML_DV:]
                hv = num / jnp.maximum(jnp.abs(den), jnp.exp(-m_t))

                m_new = m_t[last:last + 1, :]
                w_s = jnp.exp(total - bcum + ig - m_new)
                decay = jnp.exp(total + m_prev - m_new)
                wv = jnp.concatenate([w_s * v.astype(F32), w_s], axis=-1).astype(BF16)
                kv = jnp.dot(kt, wv, preferred_element_type=F32)
                c_sc[bh] = jnp.concatenate([decay, decay], axis=-1) * c_aug + kv
                m_sc[bh] = jnp.broadcast_to(m_new, m_sc.shape[1:])

                if post:
                    hv = hv + hp_ref[b, r0:r0 + L, hs]
                    hv = hv * lax.rsqrt(jnp.mean(hv * hv, axis=-1, keepdims=True) + EPS) * gain_ref[:, hs]
                    og = og_ref[b, r0:r0 + L, hs].astype(F32)
                    hv = hv / (1.0 + jnp.exp(-og))
                h_ref[b, r0:r0 + L, hs] = hv.astype(h_ref.dtype)

    @pl.when(step == pl.num_programs(0) - 1)
    def _():
        c_out_ref[...] = c_sc[...]
        m_out_ref[...] = m_sc[...]


def _ml_scan(q3, kt3, p3, gt3, gg3, state, reverse, post_args=None):
    nb, s, _ = q3.shape
    assert ML_DK == LANES and ML_DV == LANES and min(ML_CHUNK, s) <= LANES
    sub = 2 if s % (2 * ML_CHUNK) == 0 else 1
    L = min(ML_CHUNK, s) * sub
    nc = s // L
    w = ML_HEADS * ML_DK
    c0, m0 = state
    post = post_args is not None
    cidx = (lambda c: nc - 1 - c) if reverse else (lambda c: c)
    in_specs = [
        pl.BlockSpec((nb, L, w), lambda c: (0, cidx(c), 0)),
        pl.BlockSpec((nb, w, L), lambda c: (0, 0, cidx(c))),
        pl.BlockSpec((nb, L, w), lambda c: (0, cidx(c), C_MLV // w)),
        pl.BlockSpec((nb, L, LANES), lambda c: (0, cidx(c), 0)),
        pl.BlockSpec((nb, 4 * ML_HEADS, L), lambda c: (0, 0, cidx(c))),
        pl.BlockSpec(c0.shape, lambda c: (0, 0, 0)),
        pl.BlockSpec(m0.shape, lambda c: (0, 0, 0)),
    ]
    args = [q3, kt3, p3, gt3, gg3, c0, m0]
    if post:
        h_prev, gain = post_args
        in_specs += [
            pl.BlockSpec((nb, L, w), lambda c: (0, cidx(c), 0)),
            pl.BlockSpec((nb, L, w), lambda c: (0, cidx(c), C_MLO // w)),
            pl.BlockSpec((1, w), lambda c: (0, 0)),
        ]
        args += [h_prev, p3, gain]
    return pl.pallas_call(
        functools.partial(_ml_scan_kernel, nb=nb, reverse=reverse, post=post, sub=sub),
        out_shape=(jax.ShapeDtypeStruct((nb, s, w), BF16 if post else F32),
                   jax.ShapeDtypeStruct(c0.shape, F32), jax.ShapeDtypeStruct(m0.shape, F32)),
        grid=(nc,),
        in_specs=in_specs,
        out_specs=(
            pl.BlockSpec((nb, L, w), lambda c: (0, cidx(c), 0)),
            pl.BlockSpec(c0.shape, lambda c: (0, 0, 0)),
            pl.BlockSpec(m0.shape, lambda c: (0, 0, 0)),
        ),
        scratch_shapes=[pltpu.VMEM(c0.shape, F32), pltpu.VMEM(m0.shape, F32)],
        compiler_params=_cparams("arbitrary"),
        name="ml_scan",
    )(*args)


def _out_proj_kernel(da_ref, ml_ref, mla_ref, w_ref, x_ref, gate_ref, h_ref):
    n_da = da_ref.shape[1]
    n_ml = ml_ref.shape[1]
    acc = jnp.dot(da_ref[...], w_ref[0:n_da, :].astype(BF16), preferred_element_type=F32)
    acc += jnp.dot(ml_ref[...], w_ref[n_da:n_da + n_ml, :].astype(BF16), preferred_element_type=F32)
    acc += jnp.dot(mla_ref[...], w_ref[n_da + n_ml:, :].astype(BF16), preferred_element_type=F32)
    h_ref[...] = x_ref[...] + gate_ref[...] * acc


def _out_proj(da, ml, mla, w, layer, x, mod, rows_per_mod, mod_base):
    r, d = x.shape
    tm = _pick(min(r, rows_per_mod), 1024)
    tn = _pick(d, 1024)
    nt = d // tn
    return pl.pallas_call(
        _out_proj_kernel,
        out_shape=jax.ShapeDtypeStruct((r, d), F32),
        grid=(nt, r // tm),
        in_specs=[
            pl.BlockSpec((tm, da.shape[1]), lambda j, i: (i, 0)),
            pl.BlockSpec((tm, ml.shape[1]), lambda j, i: (i, 0)),
            pl.BlockSpec((tm, mla.shape[1]), lambda j, i: (i, 0)),
            pl.BlockSpec((None, w.shape[1], tn), lambda j, i: (layer, 0, j)),
            pl.BlockSpec((tm, tn), lambda j, i: (i, j)),
            pl.BlockSpec((None, 1, tn), lambda j, i: (i * tm // rows_per_mod + mod_base, 0, 2 * nt + j)),
        ],
        out_specs=pl.BlockSpec((tm, tn), lambda j, i: (i, j)),
        compiler_params=_cparams("arbitrary", "arbitrary"),
        name="out_proj",
    )(da, ml, mla, w, x, mod)


def _ffn_kernel(h_ref, mod_ref, g_ref, wg_ref, wu_ref, wd_ref, o_ref, hn_ref, *, d, rc):
    j = pl.program_id(1)

    @pl.when(j == 0)
    def _():
        shift = mod_ref[:, 3 * d:4 * d]
        scale1 = 1.0 + mod_ref[:, 4 * d:5 * d]
        _norm_rows(h_ref, g_ref, shift, scale1, hn_ref, rc)
        o_ref[...] = h_ref[...]

    hn = hn_ref[...]
    gate = jnp.dot(hn, wg_ref[...], preferred_element_type=F32)
    up = jnp.dot(hn, wu_ref[...], preferred_element_type=F32)
    act = (gate / (1.0 + jnp.exp(-gate)) * up).astype(BF16)
    o_ref[...] += mod_ref[:, 5 * d:6 * d] * jnp.dot(act, wd_ref[...], preferred_element_type=F32)


def _ffn(h, mod, g, w_gu, w_down, layer, rows_per_mod, mod_base):
    r, d = h.shape
    ff = w_down.shape[1]
    tm = _pick(min(r, rows_per_mod), 1024)
    th = 512 if ff % 512 == 0 else 256
    nh = ff // th
    return pl.pallas_call(
        functools.partial(_ffn_kernel, d=d, rc=_pick(tm, 64)),
        out_shape=jax.ShapeDtypeStruct((r, d), F32),
        grid=(r // tm, nh),
        in_specs=[
            pl.BlockSpec((tm, d), lambda i, j: (i, 0)),
            pl.BlockSpec((None, 1, 6 * d), lambda i, j: (i * tm // rows_per_mod + mod_base, 0, 0)),
            pl.BlockSpec((1, d), lambda i, j: (0, 0)),
            pl.BlockSpec((None, d, th), lambda i, j: (layer, 0, j)),
            pl.BlockSpec((None, d, th), lambda i, j: (layer, 0, nh + j)),
            pl.BlockSpec((None, th, d), lambda i, j: (layer, j, 0)),
        ],
        out_specs=pl.BlockSpec((tm, d), lambda i, j: (i, 0)),
        scratch_shapes=[pltpu.VMEM((tm, d), BF16)],
        compiler_params=_cparams("arbitrary", "arbitrary"),
        name="ffn",
    )(h, mod, g, w_gu, w_gu, w_down)


def _rope_tables(n_lat):
    pos = jnp.arange(n_lat, dtype=jnp.int32)
    r = (pos // GRID_W).astype(F32)
    c = (pos % GRID_W).astype(F32)
    half = ROPE_DIM // 4
    inv = ROPE_BASE ** (-jnp.arange(half, dtype=F32) / half)
    ar = r[:, None] * inv
    ac = c[:, None] * inv
    cos64 = jnp.concatenate([jnp.cos(ar), jnp.cos(ar), jnp.cos(ac), jnp.cos(ac)], axis=-1)
    sin64 = jnp.concatenate([-jnp.sin(ar), jnp.sin(ar), -jnp.sin(ac), jnp.sin(ac)], axis=-1)
    return cos64, sin64


def _pad_cols(a, width):
    return jnp.pad(a, ((0, 0),) * (a.ndim - 1) + ((0, width - a.shape[-1]),))


def _layout_w_in(w):
    gates0 = C_CQ
    gates1 = gates0 + 4 * ML_HEADS
    main = w[..., :gates0]
    gates = w[..., gates0:gates1]
    rest = w[..., gates1:]
    n_cq_ckv = C_KPE - C_CQ
    cq_ckv = rest[..., :n_cq_ckv]
    kpe = rest[..., n_cq_ckv:]
    return jnp.concatenate([main, cq_ckv, _pad_cols(kpe, LANES), _pad_cols(gates, LANES)], axis=-1)


def kernel(x, c, ctx, c_ctx, mod_w, mod_b, norm1_g, norm2_g, w_in, da_qk_g, da_lambda, da_out_g,
           ml_conv_w, ml_conv_b, ml_gate_b, ml_out_g, mla_q_norm_g, mla_kv_norm_g, mla_w_uq, mla_w_ukv,
           mla_qk_g, w_out, ffn_w_gu, ffn_w_down):
    batch, seq, d = x.shape
    ctx_len = ctx.shape[1]
    depth = mod_w.shape[0]
    in_dtype = x.dtype
    assert w_in.shape[2] == C_CQ + 4 * ML_HEADS + (C_KPE - C_CQ) + MLA_ROPE

    n_mod_rows = -(-(batch + 1) // 8) * 8
    cc = jnp.concatenate([c, c_ctx[None, :], jnp.zeros((n_mod_rows - batch - 1, d), F32)], axis=0)
    mod_all = _modulation(cc, mod_w, mod_b)

    cos64, sin64 = _rope_tables(seq)
    cos_da = jnp.tile(cos64, (1, 2))
    sin_da = jnp.tile(sin64, (1, 2))
    cos_mla = jnp.concatenate([cos64, jnp.ones_like(cos64)], axis=-1)
    sin_mla = jnp.concatenate([sin64, jnp.zeros_like(sin64)], axis=-1)
    tab_rows = _pick(batch * ctx_len, 1024)
    cos_id = jnp.ones((tab_rows, LANES), F32)
    sin_id = jnp.zeros((tab_rows, LANES), F32)
    lane = jnp.arange(LANES)
    group_mean = jnp.where((lane[:, None] // DA_HALF) == (lane[None, :] // DA_HALF),
                           1.0 / DA_HALF, 0.0).astype(BF16)

    w_in_b = _layout_w_in(w_in.astype(BF16))
    w_gu_b = ffn_w_gu.astype(BF16)
    w_down_b = ffn_w_down.astype(BF16)
    wq_b = jnp.pad(mla_w_uq.reshape(depth, -1, MLA_HEADS, MLA_QK),
                   ((0, 0), (0, 0), (0, 0), (0, MLA_PAD - MLA_QK))).reshape(depth, -1, MLA_HEADS * MLA_PAD).astype(BF16)
    wkv_b = mla_w_ukv.astype(BF16)

    xl = x.reshape(batch * seq, d)
    xc = ctx.reshape(batch * ctx_len, d)
    huge = batch * max(seq, ctx_len) * 2

    for l in range(depth):
        need_ctx = l < depth - 1
        lam_init = 0.8 - 0.6 * math.exp(-0.3 * l)
        mod = mod_all[l].reshape(n_mod_rows, 1, 6 * d)
        g1 = norm1_g[l].reshape(1, d)
        g2 = norm2_g[l].reshape(1, d)
        gq_da = jnp.tile(da_qk_g[l, 0], 2).reshape(1, LANES)
        gk_da = jnp.tile(da_qk_g[l, 1], 2).reshape(1, LANES)
        gq_mla = _pad_cols(mla_qk_g[l, 0].reshape(1, MLA_QK), MLA_PAD)
        gk_mla = _pad_cols(mla_qk_g[l, 1].reshape(1, MLA_QK), MLA_PAD)
        gcq = mla_q_norm_g[l].reshape(1, -1)
        gckv = mla_kv_norm_g[l].reshape(1, -1)
        gate_b = _pad_cols(ml_gate_b[l].reshape(1, -1), LANES)
        conv_w = ml_conv_w[l]
        conv_b = ml_conv_b[l].reshape(1, -1)
        ml_gain = ml_out_g[l].reshape(1, -1)
        da_gain = da_out_g[l].reshape(1, -1)
        lam_p = da_lambda[l]

        p_l, g_l = _in_proj(xl, mod, g1, w_in_b, l, seq, 0)
        p_c, g_c = _in_proj(xc, mod, g1, w_in_b, l, huge, batch)

        qd_l, kd_l = _da_prep(p_l, cos_da, sin_da, gq_da, gk_da, group_mean)
        qd_c, kd_c = _da_prep(p_c, cos_id, sin_id, gq_da, gk_da, group_mean)
        da_l = _attention("da", qd_l, [(kd_l, p_l, C_DAV), (kd_c, p_c, C_DAV)], batch,
                          [lam_p, da_gain], lam_init)
        qm_l, km_l, vm_l = _mla_prep(p_l, cos_mla, sin_mla, gcq, gckv, gq_mla, gk_mla, wq_b, wkv_b, l)
        qm_c, km_c, vm_c = _mla_prep(p_c, cos_id, sin_id, gcq, gckv, gq_mla, gk_mla, wq_b, wkv_b, l)
        mla_l = _attention("mla", qm_l, [(km_l, vm_l, 0), (km_c, vm_c, 0)], batch, [])

        p3_l = p_l.reshape(batch, seq, P_COLS)
        p3_c = p_c.reshape(batch, ctx_len, P_COLS)
        q_l, kt_l = _ml_conv(p3_l, conv_w, conv_b)
        q_c, kt_c = _ml_conv(p3_c, conv_w, conv_b)
        gt_l, gg_l = _ml_gates(g_l.reshape(batch, seq, LANES), gate_b)
        gt_c, gg_c = _ml_gates(g_c.reshape(batch, ctx_len, LANES), gate_b)
        s0 = (jnp.zeros((batch * ML_HEADS, ML_DK, 2 * ML_DV), F32),
              jnp.zeros((batch * ML_HEADS, 8, LANES), F32))
        h_cf, c_f, m_f = _ml_scan(q_c, kt_c, p3_c, gt_c, gg_c, s0, False)
        h_lf, _, _ = _ml_scan(q_l, kt_l, p3_l, gt_l, gg_l, (c_f, m_f), False)
        if need_ctx:
            ml_c, c_b, m_b = _ml_scan(q_c, kt_c, p3_c, gt_c, gg_c, s0, True, (h_cf, ml_gain))
        else:
            _, c_b, m_b = _ml_scan(q_c, kt_c, p3_c, gt_c, gg_c, s0, True)
        ml_l, _, _ = _ml_scan(q_l, kt_l, p3_l, gt_l, gg_l, (c_b, m_b), True, (h_lf, ml_gain))

        h_l = _out_proj(da_l, ml_l.reshape(batch * seq, -1), mla_l, w_out, l, xl, mod, seq, 0)
        xl = _ffn(h_l, mod, g2, w_gu_b, w_down_b, l, seq, 0)
        if need_ctx:
            da_c = _attention("da", qd_c, [(kd_c, p_c, C_DAV)], batch, [lam_p, da_gain], lam_init)
            mla_c = _attention("mla", qm_c, [(km_c, vm_c, 0)], batch, [])
            h_c = _out_proj(da_c, ml_c.reshape(batch * ctx_len, -1), mla_c, w_out, l, xc, mod, huge, batch)
            xc = _ffn(h_c, mod, g2, w_gu_b, w_down_b, l, huge, batch)

    return xl.reshape(batch, seq, d).astype(in_dtype)
```

```python
import functools
import math

import jax
import jax.numpy as jnp
from jax import lax
from jax.experimental import pallas as pl
from jax.experimental.pallas import tpu as pltpu

F32 = jnp.float32
BF16 = jnp.bfloat16

DA_HEADS = 4
DA_HALF = 64
DA_V = 2 * DA_HALF
ML_HEADS = 4
ML_DK = 128
ML_DV = 128
MLA_HEADS = 8
MLA_NOPE = 128
MLA_ROPE = 64
MLA_V = 128
MLA_QK = MLA_NOPE + MLA_ROPE
MLA_PAD = 256
GRID_W = 64
ROPE_DIM = 64
ROPE_BASE = 10000.0
EPS = 1e-6
LOG2E = 1.4426950408889634

LANES = 128
ML_CHUNK = 128
FFN_TH = 512
VMEM_LIMIT = 56 * 1024 * 1024

C_DAQ, C_DAK, C_DAV = 0, 512, 1024
C_MLQ, C_MLK, C_MLV, C_MLO = 1536, 2048, 2560, 3072
C_CQ, C_CKV, C_KPE, C_GATE = 3584, 4096, 4352, 4480
P_COLS = 4608


def _cparams(*sem):
    return pltpu.CompilerParams(dimension_semantics=sem, vmem_limit_bytes=VMEM_LIMIT)


def _pick(n, pref):
    if n <= pref:
        return n
    t = pref
    while n % t:
        t //= 2
    return t


def _mod_kernel(c_ref, w_ref, b_ref, o_ref):
    c = c_ref[...]
    s = (c / (1.0 + jnp.exp(-c))).astype(BF16)
    o_ref[...] = jnp.dot(s, w_ref[...].astype(BF16), preferred_element_type=F32) + b_ref[...]


def _modulation(cc, mod_w, mod_b):
    depth, d, n = mod_w.shape
    tn = _pick(n, 1024)
    return pl.pallas_call(
        _mod_kernel,
        out_shape=jax.ShapeDtypeStruct((depth, cc.shape[0], n), F32),
        grid=(depth, n // tn),
        in_specs=[
            pl.BlockSpec((cc.shape[0], d), lambda l, j: (0, 0)),
            pl.BlockSpec((None, d, tn), lambda l, j: (l, 0, j)),
            pl.BlockSpec((None, 1, tn), lambda l, j: (l, 0, j)),
        ],
        out_specs=pl.BlockSpec((None, cc.shape[0], tn), lambda l, j: (l, 0, j)),
        compiler_params=_cparams("arbitrary", "arbitrary"),
        name="modulation",
    )(cc, mod_w, mod_b.reshape(depth, 1, n))


def _norm_rows(x_ref, g_ref, shift, scale1, out_ref, rc):
    tm = x_ref.shape[0]
    gs = g_ref[...] * scale1

    def body(r, carry):
        rows = pl.ds(pl.multiple_of(r * rc, rc), rc)
        x = x_ref[rows, :]
        ms = jnp.mean(x * x, axis=-1, keepdims=True)
        out_ref[rows, :] = (x * lax.rsqrt(ms + EPS) * gs + shift).astype(out_ref.dtype)
        return carry

    lax.fori_loop(0, tm // rc, body, 0, unroll=2)


def _in_proj_kernel(x_ref, mod_ref, g_ref, w_ref, p_ref, gate_ref, xn_ref, *, d, rc):
    n = pl.program_id(1)

    @pl.when(n == 0)
    def _():
        shift = mod_ref[:, 0:d]
        scale1 = 1.0 + mod_ref[:, d:2 * d]
        _norm_rows(x_ref, g_ref, shift, scale1, xn_ref, rc)

    acc = jnp.dot(xn_ref[...], w_ref[...], preferred_element_type=F32)
    p_ref[...] = acc.astype(p_ref.dtype)

    @pl.when(n == pl.num_programs(1) - 1)
    def _():
        gate_ref[...] = acc[:, acc.shape[1] - LANES:]


def _in_proj(x, mod, g, w, layer, rows_per_mod, mod_base):
    r, d = x.shape
    n = w.shape[2]
    tm = _pick(min(r, rows_per_mod), 1024)
    tn = 1536
    assert n % tn == 0
    return pl.pallas_call(
        functools.partial(_in_proj_kernel, d=d, rc=_pick(tm, 64)),
        out_shape=(jax.ShapeDtypeStruct((r, n), BF16), jax.ShapeDtypeStruct((r, LANES), F32)),
        grid=(r // tm, n // tn),
        in_specs=[
            pl.BlockSpec((tm, d), lambda i, j: (i, 0)),
            pl.BlockSpec((None, 1, 6 * d), lambda i, j: (i * tm // rows_per_mod + mod_base, 0, 0)),
            pl.BlockSpec((1, d), lambda i, j: (0, 0)),
            pl.BlockSpec((None, d, tn), lambda i, j: (layer, 0, j)),
        ],
        out_specs=(
            pl.BlockSpec((tm, tn), lambda i, j: (i, j)),
            pl.BlockSpec((tm, LANES), lambda i, j: (i, 0)),
        ),
        scratch_shapes=[pltpu.VMEM((tm, d), BF16)],
        compiler_params=_cparams("arbitrary", "arbitrary"),
        name="in_proj",
    )(x, mod, g, w)


def _swap16(x):
    lane = lax.broadcasted_iota(jnp.int32, x.shape, 1)
    fwd = pltpu.roll(x, LANES - 16, 1)
    bwd = pltpu.roll(x, 16, 1)
    return jnp.where((lane % 32) < 16, fwd, bwd)


def _rope(x, cos, sin):
    return x * cos + _swap16(x) * sin


def _split3(x):
    hi = x.astype(BF16)
    r1 = x - hi.astype(F32)
    mid = r1.astype(BF16)
    lo = (r1 - mid.astype(F32)).astype(BF16)
    return hi, mid, lo


def _group_mean_sq(x, gm):
    x2 = x * x
    hi = x2.astype(BF16)
    lo = (x2 - hi.astype(F32)).astype(BF16)
    return jnp.dot(hi, gm, preferred_element_type=F32) + jnp.dot(lo, gm, preferred_element_type=F32)


def _da_prep_kernel(q_ref, k_ref, cos_ref, sin_ref, gq_ref, gk_ref, gm_ref, qo_ref, ko_ref, *, qscale):
    cos = cos_ref[...]
    sin = sin_ref[...]
    gm = gm_ref[...]
    lane = lax.broadcasted_iota(jnp.int32, cos.shape, 1)
    first = lane < DA_HALF
    for h in range(DA_HEADS):
        sl = slice(h * LANES, (h + 1) * LANES)
        q = q_ref[:, sl].astype(F32)
        q = q * lax.rsqrt(_group_mean_sq(q, gm) + EPS) * gq_ref[...]
        q = _rope(q, cos, sin) * qscale
        qo_ref[:, 2 * h * LANES:(2 * h + 1) * LANES] = jnp.where(first, q, 0.0).astype(qo_ref.dtype)
        qo_ref[:, (2 * h + 1) * LANES:(2 * h + 2) * LANES] = jnp.where(first, 0.0, q).astype(qo_ref.dtype)
        k = k_ref[:, sl].astype(F32)
        k = k * lax.rsqrt(_group_mean_sq(k, gm) + EPS) * gk_ref[...]
        ko_ref[:, sl] = _rope(k, cos, sin).astype(ko_ref.dtype)


def _da_prep(p, cos, sin, gq, gk, gm):
    r = p.shape[0]
    tm = _pick(min(r, cos.shape[0]), 1024)
    tab_blocks = cos.shape[0] // tm
    w = DA_HEADS * LANES
    qscale = DA_HALF ** -0.5 * LOG2E
    return pl.pallas_call(
        functools.partial(_da_prep_kernel, qscale=qscale),
        out_shape=(jax.ShapeDtypeStruct((r, 2 * w), BF16), jax.ShapeDtypeStruct((r, w), BF16)),
        grid=(r // tm,),
        in_specs=[
            pl.BlockSpec((tm, w), lambda i: (i, C_DAQ // w)),
            pl.BlockSpec((tm, w), lambda i: (i, C_DAK // w)),
            pl.BlockSpec((tm, LANES), lambda i: (i % tab_blocks, 0)),
            pl.BlockSpec((tm, LANES), lambda i: (i % tab_blocks, 0)),
            pl.BlockSpec((1, LANES), lambda i: (0, 0)),
            pl.BlockSpec((1, LANES), lambda i: (0, 0)),
            pl.BlockSpec((LANES, LANES), lambda i: (0, 0)),
        ],
        out_specs=(
            pl.BlockSpec((tm, 2 * w), lambda i: (i, 0)),
            pl.BlockSpec((tm, w), lambda i: (i, 0)),
        ),
        compiler_params=_cparams("arbitrary"),
        name="da_prep",
    )(p, p, cos, sin, gq, gk, gm)


def _mla_prep_kernel(cq_ref, ckv_ref, kpe_ref, cos_ref, sin_ref, gcq_ref, gckv_ref, gq_ref, gk_ref,
                     wq_ref, wkv_ref, qo_ref, ko_ref, vo_ref, *, qscale):
    cos = cos_ref[...]
    sin = sin_ref[...]
    lane = lax.broadcasted_iota(jnp.int32, cos.shape, 1)
    inv_qk = 1.0 / MLA_QK

    cq = cq_ref[...].astype(F32)
    cq = cq * lax.rsqrt(jnp.mean(cq * cq, axis=-1, keepdims=True) + EPS) * gcq_ref[...]
    qf = jnp.dot(cq.astype(BF16), wq_ref[...], preferred_element_type=F32)

    ckv = ckv_ref[...].astype(F32)
    ckv = ckv * lax.rsqrt(jnp.mean(ckv * ckv, axis=-1, keepdims=True) + EPS) * gckv_ref[...]
    kvf = jnp.dot(ckv.astype(BF16), wkv_ref[...], preferred_element_type=F32)

    kpe = jnp.where(lane < MLA_ROPE, kpe_ref[...].astype(F32), 0.0)
    kpe_sq = kpe * kpe
    kpe_rot = _rope(kpe * gk_ref[:, LANES:], cos, sin)

    for h in range(MLA_HEADS):
        q1 = qf[:, h * MLA_PAD:h * MLA_PAD + LANES]
        q2 = qf[:, h * MLA_PAD + LANES:(h + 1) * MLA_PAD]
        ms = jnp.sum(q1 * q1 + q2 * q2, axis=-1, keepdims=True) * inv_qk
        rq = lax.rsqrt(ms + EPS) * qscale
        qo_ref[:, h * MLA_PAD:h * MLA_PAD + LANES] = (q1 * rq * gq_ref[:, :LANES]).astype(qo_ref.dtype)
        q2 = _rope(q2 * rq * gq_ref[:, LANES:], cos, sin)
        qo_ref[:, h * MLA_PAD + LANES:(h + 1) * MLA_PAD] = q2.astype(qo_ref.dtype)

        kn = kvf[:, h * MLA_PAD:h * MLA_PAD + LANES]
        ms = jnp.sum(kn * kn + kpe_sq, axis=-1, keepdims=True) * inv_qk
        rk = lax.rsqrt(ms + EPS)
        ko_ref[:, h * MLA_PAD:h * MLA_PAD + LANES] = (kn * rk * gk_ref[:, :LANES]).astype(ko_ref.dtype)
        ko_ref[:, h * MLA_PAD + LANES:(h + 1) * MLA_PAD] = (kpe_rot * rk).astype(ko_ref.dtype)
        vo_ref[:, h * MLA_V:(h + 1) * MLA_V] = kvf[:, h * MLA_PAD + LANES:(h + 1) * MLA_PAD].astype(vo_ref.dtype)


def _mla_prep(p, cos, sin, gcq, gckv, gq, gk, wq, wkv, layer):
    r = p.shape[0]
    tm = _pick(min(r, cos.shape[0]), 1024)
    tab_blocks = cos.shape[0] // tm
    q_rank = wq.shape[1]
    kv_rank = wkv.shape[1]
    wide = MLA_HEADS * MLA_PAD
    qscale = MLA_QK ** -0.5 * LOG2E
    const = lambda i: (0, 0)
    return pl.pallas_call(
        functools.partial(_mla_prep_kernel, qscale=qscale),
        out_shape=(jax.ShapeDtypeStruct((r, wide), BF16), jax.ShapeDtypeStruct((r, wide), BF16),
                   jax.ShapeDtypeStruct((r, MLA_HEADS * MLA_V), BF16)),
        grid=(r // tm,),
        in_specs=[
            pl.BlockSpec((tm, q_rank), lambda i: (i, C_CQ // q_rank)),
            pl.BlockSpec((tm, kv_rank), lambda i: (i, C_CKV // kv_rank)),
            pl.BlockSpec((tm, LANES), lambda i: (i, C_KPE // LANES)),
            pl.BlockSpec((tm, LANES), lambda i: (i % tab_blocks, 0)),
            pl.BlockSpec((tm, LANES), lambda i: (i % tab_blocks, 0)),
            pl.BlockSpec((1, q_rank), const),
            pl.BlockSpec((1, kv_rank), const),
            pl.BlockSpec((1, MLA_PAD), const),
            pl.BlockSpec((1, MLA_PAD), const),
            pl.BlockSpec((None, q_rank, wide), lambda i: (layer, 0, 0)),
            pl.BlockSpec((None, kv_rank, wide), lambda i: (layer, 0, 0)),
        ],
        out_specs=(
            pl.BlockSpec((tm, wide), lambda i: (i, 0)),
            pl.BlockSpec((tm, wide), lambda i: (i, 0)),
            pl.BlockSpec((tm, MLA_HEADS * MLA_V), lambda i: (i, 0)),
        ),
        compiler_params=_cparams("arbitrary"),
        name="mla_prep",
    )(p, p, p, cos, sin, gcq, gckv, gq, gk, wq, wkv)


def _attn_core(q_ref, k_refs, v_refs, n_comp, dq, kc):
    chunks = []
    for ri, kr in enumerate(k_refs):
        rows = kr.shape[0]
        step = min(kc, rows)
        for r0 in range(0, rows, step):
            chunks.append((ri, r0, step))
    outs = []
    for c in range(n_comp):
        q = q_ref[:, c * dq:(c + 1) * dq]
        m = acc = None
        for ri, r0, step in chunks:
            s = lax.dot_general(q, k_refs[ri][r0:r0 + step, :], (((1,), (1,)), ((), ())),
                                preferred_element_type=F32)
            cm = jnp.max(s, axis=-1, keepdims=True)
            m_new = cm if m is None else jnp.maximum(m, cm)
            p = jnp.exp2(s - m_new)
            vc = v_refs[ri][r0:r0 + step, :]
            pv = jnp.dot(p.astype(BF16), jnp.concatenate([vc, jnp.ones_like(vc)], axis=-1),
                         preferred_element_type=F32)
            acc = pv if m is None else jnp.exp2(m - m_new) * acc + pv
            m = m_new
        dv = acc.shape[1] // 2
        outs.append(acc[:, :dv] / acc[:, dv:])
    return outs


def _da_attn_kernel(*refs, n_kv, lam_init, kc):
    q_ref = refs[0]
    k_refs = refs[1:1 + n_kv]
    v_refs = refs[1 + n_kv:1 + 2 * n_kv]
    lam_ref, g_ref, o_ref = refs[1 + 2 * n_kv:]
    lp = lam_ref[...]
    lam = (jnp.exp(jnp.sum(lp[0:1] * lp[1:2], axis=-1, keepdims=True))
           - jnp.exp(jnp.sum(lp[2:3] * lp[3:4], axis=-1, keepdims=True)) + lam_init)
    o0, o1 = _attn_core(q_ref, k_refs, v_refs, 2, LANES, kc)
    o = o0 - lam * o1
    o = o * lax.rsqrt(jnp.mean(o * o, axis=-1, keepdims=True) + EPS) * g_ref[...]
    o_ref[...] = (o * (1.0 - lam_init)).astype(o_ref.dtype)


def _mla_attn_kernel(*refs, n_kv, kc):
    q_ref = refs[0]
    k_refs = refs[1:1 + n_kv]
    v_refs = refs[1 + n_kv:1 + 2 * n_kv]
    (o_ref,) = refs[1 + 2 * n_kv:]
    (o,) = _attn_core(q_ref, k_refs, v_refs, 1, MLA_PAD, kc)
    o_ref[...] = o.astype(o_ref.dtype)


def _attention(kind, q, kvs, batch, extra, lam_init=0.0):
    heads = DA_HEADS if kind == "da" else MLA_HEADS
    qw = 2 * LANES if kind == "da" else MLA_PAD
    dq = LANES if kind == "da" else MLA_PAD
    dv = DA_V if kind == "da" else MLA_V
    rq = q.shape[0]
    tq = _pick(rq // batch, 2048)
    nq = rq // batch // tq
    kc = 256
    k_specs, v_specs, k_args, v_args = [], [], [], []
    for k, v, vcol in kvs:
        kl = k.shape[0] // batch
        k_specs.append(pl.BlockSpec((kl, dq), lambda b, h, i: (b, h)))
        v_specs.append(pl.BlockSpec((kl, dv), lambda b, h, i, vcol=vcol: (b, vcol // dv + h)))
        k_args.append(k)
        v_args.append(v)
    extra_specs = [pl.BlockSpec(e.shape, lambda b, h, i: (0, 0)) for e in extra]
    if kind == "da":
        body = functools.partial(_da_attn_kernel, n_kv=len(kvs), lam_init=lam_init, kc=kc)
    else:
        body = functools.partial(_mla_attn_kernel, n_kv=len(kvs), kc=kc)
    return pl.pallas_call(
        body,
        out_shape=jax.ShapeDtypeStruct((rq, heads * dv), BF16),
        grid=(batch, heads, nq),
        in_specs=[pl.BlockSpec((tq, qw), lambda b, h, i: (b * nq + i, h))] + k_specs + v_specs + extra_specs,
        out_specs=pl.BlockSpec((tq, dv), lambda b, h, i: (b * nq + i, h)),
        compiler_params=_cparams("arbitrary", "arbitrary", "arbitrary"),
        name=kind + "_attn",
    )(q, *k_args, *v_args, *extra)


def _ml_conv_kernel(xq_ref, xk_ref, wq_ref, wk_ref, bq_ref, bk_ref, q_ref, kt_ref, pad_ref, *, kscale, tc):
    s = xq_ref.shape[0]
    zeros = jnp.zeros((8, LANES), F32)
    pad_ref[0:8, :] = zeros
    pad_ref[s + 8:s + 16, :] = zeros

    def conv_silu(x_ref, w_ref, b_ref):
        x = x_ref[...].astype(F32)
        pad_ref[8:s + 8, :] = x
        prev = pad_ref[7:s + 7, :]
        nxt = pad_ref[9:s + 9, :]
        y = b_ref[...] + prev * w_ref[0:1, :] + x * w_ref[1:2, :] + nxt * w_ref[2:3, :]
        return y / (1.0 + jnp.exp(-y))

    q_ref[...] = conv_silu(xq_ref, wq_ref, bq_ref).astype(q_ref.dtype)
    k = conv_silu(xk_ref, wk_ref, bk_ref) * kscale
    for r0 in range(0, s, tc):
        kt_ref[:, r0:r0 + tc] = k[r0:r0 + tc, :].T.astype(kt_ref.dtype)


def _ml_conv(p3, conv_w, conv_b):
    nseq, s, _ = p3.shape
    w = ML_HEADS * ML_DK
    q0 = C_MLQ // LANES
    k0 = C_MLK // LANES
    return pl.pallas_call(
        functools.partial(_ml_conv_kernel, kscale=ML_DK ** -0.5, tc=_pick(s, 512)),
        out_shape=(jax.ShapeDtypeStruct((nseq, s, w), BF16), jax.ShapeDtypeStruct((nseq, w, s), BF16)),
        grid=(nseq, ML_HEADS),
        in_specs=[
            pl.BlockSpec((None, s, LANES), lambda b, h: (b, 0, q0 + h)),
            pl.BlockSpec((None, s, LANES), lambda b, h: (b, 0, k0 + h)),
            pl.BlockSpec((3, LANES), lambda b, h: (0, h)),
            pl.BlockSpec((3, LANES), lambda b, h: (0, ML_HEADS + h)),
            pl.BlockSpec((1, LANES), lambda b, h: (0, h)),
            pl.BlockSpec((1, LANES), lambda b, h: (0, ML_HEADS + h)),
        ],
        out_specs=(
            pl.BlockSpec((None, s, LANES), lambda b, h: (b, 0, h)),
            pl.BlockSpec((None, ML_DK, s), lambda b, h: (b, h, 0)),
        ),
        scratch_shapes=[pltpu.VMEM((s + 16, LANES), F32)],
        compiler_params=_cparams("arbitrary", "arbitrary"),
        name="ml_conv",
    )(p3, p3, conv_w, conv_w, conv_b, conv_b)


def _ml_gate_kernel(g_ref, b_ref, gt_ref, gg_ref, *, chunk):
    s = g_ref.shape[0]
    row = lax.broadcasted_iota(jnp.int32, (chunk, chunk), 0)
    col = lax.broadcasted_iota(jnp.int32, (chunk, chunk), 1)
    prefix = jnp.where(col <= row, 1.0, 0.0).astype(BF16)
    suffix = jnp.where(col >= row, 1.0, 0.0).astype(BF16)
    lane = lax.broadcasted_iota(jnp.int32, (chunk, LANES), 1)
    is_forget = ((lane % (2 * ML_HEADS)) >= ML_HEADS) & (lane < 4 * ML_HEADS)
    is_fwd = lane < 2 * ML_HEADS
    for r0 in range(0, s, chunk):
        g = g_ref[r0:r0 + chunk, :] + b_ref[...]
        lf = jnp.minimum(g, 0.0) - jnp.log(1.0 + jnp.exp(-jnp.abs(g)))
        parts = _split3(jnp.where(is_forget, lf, 0.0))
        cum_f = sum(jnp.dot(prefix, part, preferred_element_type=F32) for part in parts)
        cum_b = sum(jnp.dot(suffix, part, preferred_element_type=F32) for part in parts)
        out = jnp.where(is_forget, jnp.where(is_fwd, cum_f, cum_b), g)
        gt_ref[r0:r0 + chunk, :] = out
        gg_ref[:, r0:r0 + chunk] = out.T[0:4 * ML_HEADS, :]


def _ml_gates(g3, gate_b):
    nseq, s, _ = g3.shape
    return pl.pallas_call(
        functools.partial(_ml_gate_kernel, chunk=min(ML_CHUNK, s)),
        out_shape=(jax.ShapeDtypeStruct((nseq, s, LANES), F32),
                   jax.ShapeDtypeStruct((nseq, 4 * ML_HEADS, s), F32)),
        grid=(nseq,),
        in_specs=[
            pl.BlockSpec((None, s, LANES), lambda b: (b, 0, 0)),
            pl.BlockSpec((1, LANES), lambda b: (0, 0)),
        ],
        out_specs=(
            pl.BlockSpec((None, s, LANES), lambda b: (b, 0, 0)),
            pl.BlockSpec((None, 4 * ML_HEADS, s), lambda b: (b, 0, 0)),
        ),
        compiler_params=_cparams("arbitrary"),
        name="ml_gates",
    )(g3, gate_b)


def _ml_scan_kernel(*refs, nb, reverse, post):
    if post:
        (q_ref, kt_ref, v_ref, gt_ref, gg_ref, c0_ref, m0_ref, hp_ref, og_ref, gain_ref,
         h_ref, c_out_ref, m_out_ref, c_sc, m_sc) = refs
    else:
        (q_ref, kt_ref, v_ref, gt_ref, gg_ref, c0_ref, m0_ref,
         h_ref, c_out_ref, m_out_ref, c_sc, m_sc) = refs
    step = pl.program_id(0)
    L = q_ref.shape[1]
    nh = ML_HEADS

    @pl.when(step == 0)
    def _():
        c_sc[...] = c0_ref[...]
        m_sc[...] = m0_ref[...]

    row = lax.broadcasted_iota(jnp.int32, (L, L), 0)
    col = lax.broadcasted_iota(jnp.int32, (L, L), 1)
    incl = (col >= row) if reverse else (col <= row)
    last = 0 if reverse else L - 1

    def lanes(h):
        ig_lane = (2 * nh if reverse else 0) + h
        return ig_lane, ig_lane + nh

    col_rep = {}
    for b in range(nb):
        gt = gt_ref[b]
        for h in range(nh):
            ig_lane, lf_lane = lanes(h)
            col_rep[b, h] = (jnp.broadcast_to(gt[:, lf_lane:lf_lane + 1], (L, LANES)),
                             jnp.broadcast_to(gt[:, ig_lane:ig_lane + 1], (L, LANES)))

    for b in range(nb):
        gg = gg_ref[b]
        for h in range(nh):
            bh = b * nh + h
            ig_lane, lf_lane = lanes(h)
            hs = slice(h * ML_DK, (h + 1) * ML_DK)
            q = q_ref[b, :, hs]
            kt = kt_ref[b, hs, :]
            v = v_ref[b, :, hs]
            c_aug = c_sc[bh]
            m_prev = m_sc[bh][0:1, :]

            bcum, ig = col_rep[b, h]
            bcum_row = gg[lf_lane:lf_lane + 1, :]
            ig_row = gg[ig_lane:ig_lane + 1, :]
            total = bcum[last:last + 1, :]

            log_d = jnp.where(incl, bcum[:, :L] - bcum_row + ig_row, -jnp.inf)
            log_prev = bcum + m_prev
            m_t = jnp.maximum(log_prev, jnp.max(log_d, axis=-1, keepdims=True))
            dmat = jnp.exp(log_d - m_t[:, :L])
            w_prev = jnp.exp(log_prev - m_t)
            s = jnp.dot(q, kt, preferred_element_type=F32) * dmat
            lhs = jnp.concatenate([s.astype(BF16), (w_prev * q.astype(F32)).astype(BF16)], axis=-1)
            rhs = jnp.concatenate([jnp.concatenate([v, jnp.ones_like(v)], axis=-1), c_aug.astype(BF16)], axis=0)
            tot = jnp.dot(lhs, rhs, preferred_element_type=F32)
            num = tot[:, :ML_DV]
            den = tot[:, ML_DV:]
            hv = num / jnp.maximum(jnp.abs(den), jnp.exp(-m_t))

            m_new = m_t[last:last + 1, :]
            w_s = jnp.exp(total - bcum + ig - m_new)
            decay = jnp.exp(total + m_prev - m_new)
            wv = jnp.concatenate([w_s * v.astype(F32), w_s], axis=-1).astype(BF16)
            kv = jnp.dot(kt, wv, preferred_element_type=F32)
            c_sc[bh] = jnp.concatenate([decay, decay], axis=-1) * c_aug + kv
            m_sc[bh] = jnp.broadcast_to(m_new, m_sc.shape[1:])

            if post:
                hv = hv + hp_ref[b, :, hs]
                hv = hv * lax.rsqrt(jnp.mean(hv * hv, axis=-1, keepdims=True) + EPS) * gain_ref[:, hs]
                og = og_ref[b, :, hs].astype(F32)
                hv = hv / (1.0 + jnp.exp(-og))
            h_ref[b, :, hs] = hv.astype(h_ref.dtype)

    @pl.when(step == pl.num_programs(0) - 1)
    def _():
        c_out_ref[...] = c_sc[...]
        m_out_ref[...] = m_sc[...]


def _ml_scan(q3, kt3, p3, gt3, gg3, state, reverse, post_args=None):
    nb, s, _ = q3.shape
    assert ML_DK == LANES and ML_DV == LANES and min(ML_CHUNK, s) <= LANES
    L = min(ML_CHUNK, s)
    nc = s // L
    w = ML_HEADS * ML_DK
    c0, m0 = state
    post = post_args is not None
    cidx = (lambda c: nc - 1 - c) if reverse else (lambda c: c)
    in_specs = [
        pl.BlockSpec((nb, L, w), lambda c: (0, cidx(c), 0)),
        pl.BlockSpec((nb, w, L), lambda c: (0, 0, cidx(c))),
        pl.BlockSpec((nb, L, w), lambda c: (0, cidx(c), C_MLV // w)),
        pl.BlockSpec((nb, L, LANES), lambda c: (0, cidx(c), 0)),
        pl.BlockSpec((nb, 4 * ML_HEADS, L), lambda c: (0, 0, cidx(c))),
        pl.BlockSpec(c0.shape, lambda c: (0, 0, 0)),
        pl.BlockSpec(m0.shape, lambda c: (0, 0, 0)),
    ]
    args = [q3, kt3, p3, gt3, gg3, c0, m0]
    if post:
        h_prev, gain = post_args
        in_specs += [
            pl.BlockSpec((nb, L, w), lambda c: (0, cidx(c), 0)),
            pl.BlockSpec((nb, L, w), lambda c: (0, cidx(c), C_MLO // w)),
            pl.BlockSpec((1, w), lambda c: (0, 0)),
        ]
        args += [h_prev, p3, gain]
    return pl.pallas_call(
        functools.partial(_ml_scan_kernel, nb=nb, reverse=reverse, post=post),
        out_shape=(jax.ShapeDtypeStruct((nb, s, w), BF16 if post else F32),
                   jax.ShapeDtypeStruct(c0.shape, F32), jax.ShapeDtypeStruct(m0.shape, F32)),
        grid=(nc,),
        in_specs=in_specs,
        out_specs=(
            pl.BlockSpec((nb, L, w), lambda c: (0, cidx(c), 0)),
            pl.BlockSpec(c0.shape, lambda c: (0, 0, 0)),
            pl.BlockSpec(m0.shape, lambda c: (0, 0, 0)),
        ),
        scratch_shapes=[pltpu.VMEM(c0.shape, F32), pltpu.VMEM(m0.shape, F32)],
        compiler_params=_cparams("arbitrary"),
        name="ml_scan",
    )(*args)


def _out_proj_kernel(da_ref, ml_ref, mla_ref, w_ref, x_ref, gate_ref, h_ref):
    n_da = da_ref.shape[1]
    n_ml = ml_ref.shape[1]
    acc = jnp.dot(da_ref[...], w_ref[0:n_da, :].astype(BF16), preferred_element_type=F32)
    acc += jnp.dot(ml_ref[...], w_ref[n_da:n_da + n_ml, :].astype(BF16), preferred_element_type=F32)
    acc += jnp.dot(mla_ref[...], w_ref[n_da + n_ml:, :].astype(BF16), preferred_element_type=F32)
    h_ref[...] = x_ref[...] + gate_ref[...] * acc


def _out_proj(da, ml, mla, w, layer, x, mod, rows_per_mod, mod_base):
    r, d = x.shape
    tm = _pick(min(r, rows_per_mod), 1024)
    tn = _pick(d, 1024)
    nt = d // tn
    return pl.pallas_call(
        _out_proj_kernel,
        out_shape=jax.ShapeDtypeStruct((r, d), F32),
        grid=(nt, r // tm),
        in_specs=[
            pl.BlockSpec((tm, da.shape[1]), lambda j, i: (i, 0)),
            pl.BlockSpec((tm, ml.shape[1]), lambda j, i: (i, 0)),
            pl.BlockSpec((tm, mla.shape[1]), lambda j, i: (i, 0)),
            pl.BlockSpec((None, w.shape[1], tn), lambda j, i: (layer, 0, j)),
            pl.BlockSpec((tm, tn), lambda j, i: (i, j)),
            pl.BlockSpec((None, 1, tn), lambda j, i: (i * tm // rows_per_mod + mod_base, 0, 2 * nt + j)),
        ],
        out_specs=pl.BlockSpec((tm, tn), lambda j, i: (i, j)),
        compiler_params=_cparams("arbitrary", "arbitrary"),
        name="out_proj",
    )(da, ml, mla, w, x, mod)


def _ffn_kernel(h_ref, mod_ref, g_ref, wgu_ref, wd_ref, o_ref, hn_ref, *, d, rc):
    j = pl.program_id(1)

    @pl.when(j == 0)
    def _():
        shift = mod_ref[:, 3 * d:4 * d]
        scale1 = 1.0 + mod_ref[:, 4 * d:5 * d]
        _norm_rows(h_ref, g_ref, shift, scale1, hn_ref, rc)
        o_ref[...] = h_ref[...]

    hn = hn_ref[...]
    gu = jnp.dot(hn, wgu_ref[...], preferred_element_type=F32)
    th = gu.shape[1] // 2
    gate = gu[:, :th]
    up = gu[:, th:]
    act = (gate / (1.0 + jnp.exp(-gate)) * up).astype(BF16)
    o_ref[...] += mod_ref[:, 5 * d:6 * d] * jnp.dot(act, wd_ref[...], preferred_element_type=F32)


def _ffn(h, mod, g, w_gu, w_down, layer, rows_per_mod, mod_base):
    r, d = h.shape
    ff = w_down.shape[1]
    tm = _pick(min(r, rows_per_mod), 1024)
    th = FFN_TH if ff % FFN_TH == 0 else 256
    nh = ff // th
    return pl.pallas_call(
        functools.partial(_ffn_kernel, d=d, rc=_pick(tm, 64)),
        out_shape=jax.ShapeDtypeStruct((r, d), F32),
        grid=(r // tm, nh),
        in_specs=[
            pl.BlockSpec((tm, d), lambda i, j: (i, 0)),
            pl.BlockSpec((None, 1, 6 * d), lambda i, j: (i * tm // rows_per_mod + mod_base, 0, 0)),
            pl.BlockSpec((1, d), lambda i, j: (0, 0)),
            pl.BlockSpec((None, d, 2 * th), lambda i, j: (layer, 0, j)),
            pl.BlockSpec((None, th, d), lambda i, j: (layer, j, 0)),
        ],
        out_specs=pl.BlockSpec((tm, d), lambda i, j: (i, 0)),
        scratch_shapes=[pltpu.VMEM((tm, d), BF16)],
        compiler_params=_cparams("arbitrary", "arbitrary"),
        name="ffn",
    )(h, mod, g, w_gu, w_down)


def _rope_tables(n_lat):
    pos = jnp.arange(n_lat, dtype=jnp.int32)
    r = (pos // GRID_W).astype(F32)
    c = (pos % GRID_W).astype(F32)
    half = ROPE_DIM // 4
    inv = ROPE_BASE ** (-jnp.arange(half, dtype=F32) / half)
    ar = r[:, None] * inv
    ac = c[:, None] * inv
    cos64 = jnp.concatenate([jnp.cos(ar), jnp.cos(ar), jnp.cos(ac), jnp.cos(ac)], axis=-1)
    sin64 = jnp.concatenate([-jnp.sin(ar), jnp.sin(ar), -jnp.sin(ac), jnp.sin(ac)], axis=-1)
    return cos64, sin64


def _pad_cols(a, width):
    return jnp.pad(a, ((0, 0),) * (a.ndim - 1) + ((0, width - a.shape[-1]),))


def _layout_w_in(w):
    gates0 = C_CQ
    gates1 = gates0 + 4 * ML_HEADS
    main = w[..., :gates0]
    gates = w[..., gates0:gates1]
    rest = w[..., gates1:]
    n_cq_ckv = C_KPE - C_CQ
    cq_ckv = rest[..., :n_cq_ckv]
    kpe = rest[..., n_cq_ckv:]
    return jnp.concatenate([main, cq_ckv, _pad_cols(kpe, LANES), _pad_cols(gates, LANES)], axis=-1)


def kernel(x, c, ctx, c_ctx, mod_w, mod_b, norm1_g, norm2_g, w_in, da_qk_g, da_lambda, da_out_g,
           ml_conv_w, ml_conv_b, ml_gate_b, ml_out_g, mla_q_norm_g, mla_kv_norm_g, mla_w_uq, mla_w_ukv,
           mla_qk_g, w_out, ffn_w_gu, ffn_w_down):
    batch, seq, d = x.shape
    ctx_len = ctx.shape[1]
    depth = mod_w.shape[0]
    in_dtype = x.dtype
    assert w_in.shape[2] == C_CQ + 4 * ML_HEADS + (C_KPE - C_CQ) + MLA_ROPE

    n_mod_rows = -(-(batch + 1) // 8) * 8
    cc = jnp.concatenate([c, c_ctx[None, :], jnp.zeros((n_mod_rows - batch - 1, d), F32)], axis=0)
    mod_all = _modulation(cc, mod_w, mod_b)

    cos64, sin64 = _rope_tables(seq)
    cos_da = jnp.tile(cos64, (1, 2))
    sin_da = jnp.tile(sin64, (1, 2))
    cos_mla = jnp.concatenate([cos64, jnp.ones_like(cos64)], axis=-1)
    sin_mla = jnp.concatenate([sin64, jnp.zeros_like(sin64)], axis=-1)
    tab_rows = _pick(batch * ctx_len, 1024)
    cos_id = jnp.ones((tab_rows, LANES), F32)
    sin_id = jnp.zeros((tab_rows, LANES), F32)
    lane = jnp.arange(LANES)
    group_mean = jnp.where((lane[:, None] // DA_HALF) == (lane[None, :] // DA_HALF),
                           1.0 / DA_HALF, 0.0).astype(BF16)

    w_in_b = _layout_w_in(w_in.astype(BF16))
    ff = ffn_w_down.shape[1]
    th = FFN_TH if ff % FFN_TH == 0 else 256
    w_gu_b = ffn_w_gu.astype(BF16).reshape(depth, d, 2, ff // th, th).transpose(0, 1, 3, 2, 4).reshape(depth, d, 2 * ff)
    w_down_b = ffn_w_down.astype(BF16)
    wq_b = jnp.pad(mla_w_uq.reshape(depth, -1, MLA_HEADS, MLA_QK),
                   ((0, 0), (0, 0), (0, 0), (0, MLA_PAD - MLA_QK))).reshape(depth, -1, MLA_HEADS * MLA_PAD).astype(BF16)
    wkv_b = mla_w_ukv.astype(BF16)

    xl = x.reshape(batch * seq, d)
    xc = ctx.reshape(batch * ctx_len, d)
    huge = batch * max(seq, ctx_len) * 2

    for l in range(depth):
        need_ctx = l < depth - 1
        lam_init = 0.8 - 0.6 * math.exp(-0.3 * l)
        mod = mod_all[l].reshape(n_mod_rows, 1, 6 * d)
        g1 = norm1_g[l].reshape(1, d)
        g2 = norm2_g[l].reshape(1, d)
        gq_da = jnp.tile(da_qk_g[l, 0], 2).reshape(1, LANES)
        gk_da = jnp.tile(da_qk_g[l, 1], 2).reshape(1, LANES)
        gq_mla = _pad_cols(mla_qk_g[l, 0].reshape(1, MLA_QK), MLA_PAD)
        gk_mla = _pad_cols(mla_qk_g[l, 1].reshape(1, MLA_QK), MLA_PAD)
        gcq = mla_q_norm_g[l].reshape(1, -1)
        gckv = mla_kv_norm_g[l].reshape(1, -1)
        gate_b = _pad_cols(ml_gate_b[l].reshape(1, -1), LANES)
        conv_w = ml_conv_w[l]
        conv_b = ml_conv_b[l].reshape(1, -1)
        ml_gain = ml_out_g[l].reshape(1, -1)
        da_gain = da_out_g[l].reshape(1, -1)
        lam_p = da_lambda[l]

        p_l, g_l = _in_proj(xl, mod, g1, w_in_b, l, seq, 0)
        p_c, g_c = _in_proj(xc, mod, g1, w_in_b, l, huge, batch)

        qd_l, kd_l = _da_prep(p_l, cos_da, sin_da, gq_da, gk_da, group_mean)
        qd_c, kd_c = _da_prep(p_c, cos_id, sin_id, gq_da, gk_da, group_mean)
        da_l = _attention("da", qd_l, [(kd_l, p_l, C_DAV), (kd_c, p_c, C_DAV)], batch,
                          [lam_p, da_gain], lam_init)
        qm_l, km_l, vm_l = _mla_prep(p_l, cos_mla, sin_mla, gcq, gckv, gq_mla, gk_mla, wq_b, wkv_b, l)
        qm_c, km_c, vm_c = _mla_prep(p_c, cos_id, sin_id, gcq, gckv, gq_mla, gk_mla, wq_b, wkv_b, l)
        mla_l = _attention("mla", qm_l, [(km_l, vm_l, 0), (km_c, vm_c, 0)], batch, [])

        p3_l = p_l.reshape(batch, seq, P_COLS)
        p3_c = p_c.reshape(batch, ctx_len, P_COLS)
        q_l, kt_l = _ml_conv(p3_l, conv_w, conv_b)
        q_c, kt_c = _ml_conv(p3_c, conv_w, conv_b)
        gt_l, gg_l = _ml_gates(g_l.reshape(batch, seq, LANES), gate_b)
        gt_c, gg_c = _ml_gates(g_c.reshape(batch, ctx_len, LANES), gate_b)
        s0 = (jnp.zeros((batch * ML_HEADS, ML_DK, 2 * ML_DV), F32),
              jnp.zeros((batch * ML_HEADS, 8, LANES), F32))
        h_cf, c_f, m_f = _ml_scan(q_c, kt_c, p3_c, gt_c, gg_c, s0, False)
        h_lf, _, _ = _ml_scan(q_l, kt_l, p3_l, gt_l, gg_l, (c_f, m_f), False)
        if need_ctx:
            ml_c, c_b, m_b = _ml_scan(q_c, kt_c, p3_c, gt_c, gg_c, s0, True, (h_cf, ml_gain))
        else:
            _, c_b, m_b = _ml_scan(q_c, kt_c, p3_c, gt_c, gg_c, s0, True)
        ml_l, _, _ = _ml_scan(q_l, kt_l, p3_l, gt_l, gg_l, (c_b, m_b), True, (h_lf, ml_gain))

        h_l = _out_proj(da_l, ml_l.reshape(batch * seq, -1), mla_l, w_out, l, xl, mod, seq, 0)
        xl = _ffn(h_l, mod, g2, w_gu_b, w_down_b, l, seq, 0)
        if need_ctx:
            da_c = _attention("da", qd_c, [(kd_c, p_c, C_DAV)], batch, [lam_p, da_gain], lam_init)
            mla_c = _attention("mla", qm_c, [(km_c, vm_c, 0)], batch, [])
            h_c = _out_proj(da_c, ml_c.reshape(batch * ctx_len, -1), mla_c, w_out, l, xc, mod, huge, batch)
            xc = _ffn(h_c, mod, g2, w_gu_b, w_down_b, l, huge, batch)

    return xl.reshape(batch, seq, d).astype(in_dtype)
```
